```python
import jax, jax.numpy as jnp
from jax import lax
import numpy as np

D_MODEL = 1024
BATCH = 8
SEQ = 2048
DEPTH = 4
DEC_BATCH = 32
DEC_SEQ = 8
PAST_LEN = 8192
PAGE_SIZE = 128

HEAD_DIM = 64
N_HEADS_A = 8
N_KV_A = 2
N_IDX_HEADS = 4
D_IDX = 64
TOPK_MAX = 256
N_HEADS_B = 8
BLOCK_Q = 128
ROPE_THETA = 10000.0
D_FF = 2816
N_EXPERTS = 8
TOP_K_EXPERTS = 2
D_FF_EXPERT = 3584
LN_EPS = 1e-5
DEEPNORM_ALPHA = (2.0 * DEPTH) ** 0.25
DEEPNORM_BETA = (8.0 * DEPTH) ** -0.25
FORGET_BIAS_INIT = 3.0

WIDTH_A = N_HEADS_A * HEAD_DIM
WIDTH_B = N_HEADS_B * HEAD_DIM
IN_SPLITS = (WIDTH_A, N_KV_A * HEAD_DIM, N_KV_A * HEAD_DIM, N_IDX_HEADS * D_IDX, D_IDX, N_IDX_HEADS,
             WIDTH_B, WIDTH_B, WIDTH_B, N_HEADS_B, D_MODEL, D_MODEL)
IN_COLS = sum(IN_SPLITS)
N_DENSE = (DEPTH + 1) // 2
N_MOE = DEPTH // 2

kernel_name = "dsa_fox_gated_hybrid_decoder_step"


def layer_norm(x, g, b):
    xf = x.astype(jnp.float32)
    mu = jnp.mean(xf, axis=-1, keepdims=True)
    var = jnp.mean(jnp.square(xf - mu), axis=-1, keepdims=True)
    return ((xf - mu) * lax.rsqrt(var + LN_EPS) * g.astype(jnp.float32) + b.astype(jnp.float32)).astype(x.dtype)


def rope(x, pos):
    d = x.shape[-1]
    inv = ROPE_THETA ** (-jnp.arange(0, d, 2, dtype=jnp.float32) / d)
    ang = pos.astype(jnp.float32)[:, None] * inv[None, :]
    cos = jnp.cos(ang)[None, :, None, :]
    sin = jnp.sin(ang)[None, :, None, :]
    xf = x.astype(jnp.float32)
    x1, x2 = xf[..., : d // 2], xf[..., d // 2:]
    return jnp.concatenate([x1 * cos - x2 * sin, x1 * sin + x2 * cos], axis=-1).astype(x.dtype)


def gather_rows(a, idx):
    return jax.vmap(lambda ab, ib: ab[ib])(a, idx)


def mixer_projections(x, pos, w_in, b_f):
    B, T, _ = x.shape
    h = jnp.einsum('btd,dc->btc', x, w_in)
    cuts = [int(c) for c in np.cumsum(IN_SPLITS)[:-1]]
    q_a, k_a, v_a, q_i, k_i, w_i, q_b, k_b, v_b, f_b, g_a, g_b = jnp.split(h, cuts, axis=-1)
    q_a = rope(q_a.reshape(B, T, N_HEADS_A, HEAD_DIM), pos)
    k_a = rope(k_a.reshape(B, T, N_KV_A, HEAD_DIM), pos)
    v_a = v_a.reshape(B, T, N_KV_A, HEAD_DIM)
    q_i = rope(q_i.reshape(B, T, N_IDX_HEADS, D_IDX), pos)
    k_i = rope(k_i[:, :, None, :], pos)[:, :, 0, :]
    q_b = q_b.reshape(B, T, N_HEADS_B, HEAD_DIM)
    k_b = k_b.reshape(B, T, N_HEADS_B, HEAD_DIM)
    v_b = v_b.reshape(B, T, N_HEADS_B, HEAD_DIM)
    logf = jax.nn.log_sigmoid((f_b + b_f).astype(jnp.float32))
    return q_a, k_a, v_a, q_i, k_i, w_i, q_b, k_b, v_b, logf, g_a, g_b


def indexer_scores(q_i, w_i, k_i):
    s = jnp.einsum('bqhd,bld->bqhl', q_i, k_i, preferred_element_type=jnp.float32)
    return jnp.einsum('bqh,bqhl->bql', w_i.astype(jnp.float32), jax.nn.relu(s))


def gqa_attend(q, k_sel, v_sel, valid):
    B, Q, _, D = q.shape
    qg = q.reshape(B, Q, N_KV_A, N_HEADS_A // N_KV_A, D)
    logits = jnp.einsum('bqgrd,bqkgd->bqgrk', qg, k_sel, preferred_element_type=jnp.float32) * (D ** -0.5)
    logits = jnp.where(valid[:, :, None, None, :], logits, -jnp.inf)
    p = jax.nn.softmax(logits, axis=-1).astype(v_sel.dtype)
    o = jnp.einsum('bqgrk,bqkgd->bqgrd', p, v_sel)
    return o.reshape(B, Q, N_HEADS_A * D)


def dsa_prompt(q_a, k_a, v_a, q_i, k_i, w_i):
    B, S = q_a.shape[:2]
    topk = min(TOPK_MAX, S // 4)
    key_pos = jnp.arange(S)

    def block(i):
        start = i * BLOCK_Q
        qpos = start + jnp.arange(BLOCK_Q)
        sl = lambda a: lax.dynamic_slice_in_dim(a, start, BLOCK_Q, axis=1)
        causal = key_pos[None, :] <= qpos[:, None]
        sc = jnp.where(causal[None], indexer_scores(sl(q_i), sl(w_i), k_i), -jnp.inf)
        _, idx = lax.top_k(sc, topk)
        valid = idx <= qpos[None, :, None]
        return gqa_attend(sl(q_a), gather_rows(k_a, idx), gather_rows(v_a, idx), valid)

    out = lax.map(block, jnp.arange(S // BLOCK_Q))
    return jnp.moveaxis(out, 0, 1).reshape(B, S, WIDTH_A)


def dsa_sample(q_a, k_a, v_a, q_i, k_i, w_i, pool_k, pool_v, pool_ki, page_table):
    DB, T = q_a.shape[:2]
    P = page_table.shape[1] * PAGE_SIZE
    L = P + T
    topk = min(TOPK_MAX, L // 4)
    ki_all = jnp.concatenate([pool_ki[page_table].reshape(DB, P, D_IDX), k_i], axis=1)
    qpos = P + jnp.arange(T)
    causal = jnp.arange(L)[None, :] <= qpos[:, None]
    sc = jnp.where(causal[None], indexer_scores(q_i, w_i, ki_all), -jnp.inf)
    _, idx = lax.top_k(sc, topk)
    valid = idx <= qpos[None, :, None]
    in_past = (idx < P)[..., None, None]
    pidx = jnp.minimum(idx, P - 1)
    phys_page = jax.vmap(lambda pt, ib: pt[ib])(page_table, pidx // PAGE_SIZE)
    phys = phys_page * PAGE_SIZE + pidx % PAGE_SIZE
    nidx = jnp.clip(idx - P, 0, T - 1)
    flat_k = pool_k.reshape(-1, N_KV_A, HEAD_DIM)
    flat_v = pool_v.reshape(-1, N_KV_A, HEAD_DIM)
    k_sel = jnp.where(in_past, flat_k[phys], gather_rows(k_a, nidx))
    v_sel = jnp.where(in_past, flat_v[phys], gather_rows(v_a, nidx))
    return gqa_attend(q_a, k_sel, v_sel, valid)


def fox_core(q, k, v, c_q, c_k, causal):
    B, Q, H, D = q.shape
    logits = jnp.einsum('bqhd,blhd->bhql', q, k, preferred_element_type=jnp.float32) * (D ** -0.5)
    logits = logits + (jnp.transpose(c_q, (0, 2, 1))[:, :, :, None] - jnp.transpose(c_k, (0, 2, 1))[:, :, None, :])
    logits = jnp.where(causal[None, None], logits, -jnp.inf)
    p = jax.nn.softmax(logits, axis=-1).astype(v.dtype)
    return jnp.einsum('bhql,blhd->bqhd', p, v).reshape(B, Q, H * D)


def fox_prompt(q, k, v, logf):
    B, S, H, D = q.shape
    c = jnp.cumsum(logf, axis=1)
    key_pos = jnp.arange(S)

    def block(i):
        start = i * BLOCK_Q
        qpos = start + jnp.arange(BLOCK_Q)
        qb = lax.dynamic_slice_in_dim(q, start, BLOCK_Q, axis=1)
        cq = lax.dynamic_slice_in_dim(c, start, BLOCK_Q, axis=1)
        return fox_core(qb, k, v, cq, c, key_pos[None, :] <= qpos[:, None])

    out = lax.map(block, jnp.arange(S // BLOCK_Q))
    return jnp.moveaxis(out, 0, 1).reshape(B, S, H * D)


def fox_sample(q, k, v, logf, pool_k, pool_v, pool_lf, page_table):
    DB, T, H, D = q.shape
    P = page_table.shape[1] * PAGE_SIZE
    k_all = jnp.concatenate([pool_k[page_table].reshape(DB, P, H, D), k], axis=1)
    v_all = jnp.concatenate([pool_v[page_table].reshape(DB, P, H, D), v], axis=1)
    lf_past = pool_lf[page_table].reshape(DB, P, H).astype(jnp.float32)
    c_past = -(lax.cumsum(lf_past, axis=1, reverse=True) - lf_past)
    c_new = jnp.cumsum(logf, axis=1)
    c_k = jnp.concatenate([c_past, c_new], axis=1)
    causal = jnp.arange(P + T)[None, :] <= (P + jnp.arange(T))[:, None]
    return fox_core(q, k_all, v_all, c_new, c_k, causal)


def swiglu(x, w_gate, w_up, w_down):
    g = jnp.einsum('btd,df->btf', x, w_gate)
    u = jnp.einsum('btd,df->btf', x, w_up)
    return jnp.einsum('btf,fd->btd', jax.nn.silu(g) * u, w_down)


def moe_swiglu(x, w_router, w_gate, w_up, w_down):
    logits = jnp.einsum('btd,de->bte', x, w_router, preferred_element_type=jnp.float32)
    top_v, top_i = lax.top_k(logits, TOP_K_EXPERTS)
    top_w = jax.nn.softmax(top_v, axis=-1)
    gates = jnp.sum(jax.nn.one_hot(top_i, N_EXPERTS, dtype=jnp.float32) * top_w[..., None], axis=-2)
    y = jnp.zeros_like(x)
    for e in range(N_EXPERTS):
        y = y + gates[..., e:e + 1].astype(x.dtype) * swiglu(x, w_gate[e], w_up[e], w_down[e])
    return y


def trunk(x, pos, attend_a, attend_b, w_in, b_f, w_br_a, w_br_b, w_out, ln1_g, ln1_b, ln2_g, ln2_b,
          w_gate_d, w_up_d, w_down_d, w_router, w_gate_e, w_up_e, w_down_e):
    rows = []
    for l in range(DEPTH):
        q_a, k_a, v_a, q_i, k_i, w_i, q_b, k_b, v_b, logf, g_a, g_b = mixer_projections(x, pos, w_in[l], b_f[l])
        o_a = attend_a(l, q_a, k_a, v_a, q_i, k_i, w_i)
        o_b = attend_b(l, q_b, k_b, v_b, logf)
        u_a = jnp.einsum('btw,wd->btd', o_a, w_br_a[l])
        u_b = jnp.einsum('btw,wd->btd', o_b, w_br_b[l])
        merged = jax.nn.sigmoid(g_a) * u_a + jax.nn.sigmoid(g_b) * u_b
        mix = jnp.einsum('btd,de->bte', merged, w_out[l])
        x = layer_norm(DEEPNORM_ALPHA * x + mix, ln1_g[l], ln1_b[l])
        if l % 2 == 0:
            f = swiglu(x, w_gate_d[l // 2], w_up_d[l // 2], w_down_d[l // 2])
        else:
            f = moe_swiglu(x, w_router[l // 2], w_gate_e[l // 2], w_up_e[l // 2], w_down_e[l // 2])
        x = layer_norm(DEEPNORM_ALPHA * x + f, ln2_g[l], ln2_b[l])
        rows.append((k_a, v_a, k_i, k_b, v_b, logf))
    return x, tuple(jnp.stack(r) for r in zip(*rows))


def setup_inputs(seed: int = 0) -> dict:
    key = jax.random.key(seed)
    ks = jax.random.split(key, 32)
    f32 = jnp.float32
    nrm = lambda k, shape, scale: jax.random.normal(k, shape, f32) * scale
    n_pages = PAST_LEN // PAGE_SIZE
    n_pool = (DEC_BATCH * n_pages * 5) // 4
    offs = np.cumsum((0,) + IN_SPLITS)
    col_scale = np.ones((IN_COLS,), np.float32)
    col_scale[offs[2]:offs[3]] = DEEPNORM_BETA
    col_scale[offs[8]:offs[9]] = DEEPNORM_BETA
    page_table = jax.random.permutation(ks[0], n_pool)[: DEC_BATCH * n_pages].reshape(DEC_BATCH, n_pages).astype(jnp.int32)
    return {
        'x_prompt': nrm(ks[1], (BATCH, SEQ, D_MODEL), 1.0),
        'x_sample': nrm(ks[2], (DEC_BATCH, DEC_SEQ, D_MODEL), 1.0),
        'cache_k_a': nrm(ks[3], (DEPTH, n_pool, PAGE_SIZE, N_KV_A, HEAD_DIM), 1.0),
        'cache_v_a': nrm(ks[4], (DEPTH, n_pool, PAGE_SIZE, N_KV_A, HEAD_DIM), 1.0),
        'cache_kidx_a': nrm(ks[5], (DEPTH, n_pool, PAGE_SIZE, D_IDX), 1.0),
        'cache_k_b': nrm(ks[6], (DEPTH, n_pool, PAGE_SIZE, N_HEADS_B, HEAD_DIM), 1.0),
        'cache_v_b': nrm(ks[7], (DEPTH, n_pool, PAGE_SIZE, N_HEADS_B, HEAD_DIM), 1.0),
        'cache_logf_b': jax.nn.log_sigmoid(FORGET_BIAS_INIT + nrm(ks[8], (DEPTH, n_pool, PAGE_SIZE, N_HEADS_B), 0.5)),
        'page_table': page_table,
        'w_in': nrm(ks[9], (DEPTH, D_MODEL, IN_COLS), D_MODEL ** -0.5) * jnp.asarray(col_scale),
        'b_f': FORGET_BIAS_INIT + nrm(ks[10], (DEPTH, N_HEADS_B), 0.1),
        'w_br_a': nrm(ks[11], (DEPTH, WIDTH_A, D_MODEL), WIDTH_A ** -0.5 * DEEPNORM_BETA),
        'w_br_b': nrm(ks[12], (DEPTH, WIDTH_B, D_MODEL), WIDTH_B ** -0.5 * DEEPNORM_BETA),
        'w_out': nrm(ks[13], (DEPTH, D_MODEL, D_MODEL), D_MODEL ** -0.5 * DEEPNORM_BETA),
        'ln1_g': 1.0 + nrm(ks[14], (DEPTH, D_MODEL), 0.02),
        'ln1_b': nrm(ks[15], (DEPTH, D_MODEL), 0.02),
        'ln2_g': 1.0 + nrm(ks[16], (DEPTH, D_MODEL), 0.02),
        'ln2_b': nrm(ks[17], (DEPTH, D_MODEL), 0.02),
        'w_gate_d': nrm(ks[18], (N_DENSE, D_MODEL, D_FF), D_MODEL ** -0.5),
        'w_up_d': nrm(ks[19], (N_DENSE, D_MODEL, D_FF), D_MODEL ** -0.5 * DEEPNORM_BETA),
        'w_down_d': nrm(ks[20], (N_DENSE, D_FF, D_MODEL), D_FF ** -0.5 * DEEPNORM_BETA),
        'w_router': nrm(ks[21], (N_MOE, D_MODEL, N_EXPERTS), D_MODEL ** -0.5),
        'w_gate_e': nrm(ks[22], (N_MOE, N_EXPERTS, D_MODEL, D_FF_EXPERT), D_MODEL ** -0.5),
        'w_up_e': nrm(ks[23], (N_MOE, N_EXPERTS, D_MODEL, D_FF_EXPERT), D_MODEL ** -0.5 * DEEPNORM_BETA),
        'w_down_e': nrm(ks[24], (N_MOE, N_EXPERTS, D_FF_EXPERT, D_MODEL), D_FF_EXPERT ** -0.5 * DEEPNORM_BETA),
    }


def reference(x_prompt, x_sample, cache_k_a, cache_v_a, cache_kidx_a, cache_k_b, cache_v_b, cache_logf_b,
              page_table, w_in, b_f, w_br_a, w_br_b, w_out, ln1_g, ln1_b, ln2_g, ln2_b,
              w_gate_d, w_up_d, w_down_d, w_router, w_gate_e, w_up_e, w_down_e):
    S = x_prompt.shape[1]
    T = x_sample.shape[1]
    P = page_table.shape[1] * PAGE_SIZE

    def a_prompt(l, q, k, v, qi, ki, wi):
        return dsa_prompt(q, k, v, qi, ki, wi)

    def b_prompt(l, q, k, v, lf):
        return fox_prompt(q, k, v, lf)

    def a_sample(l, q, k, v, qi, ki, wi):
        return dsa_sample(q, k, v, qi, ki, wi, cache_k_a[l], cache_v_a[l], cache_kidx_a[l], page_table)

    def b_sample(l, q, k, v, lf):
        return fox_sample(q, k, v, lf, cache_k_b[l], cache_v_b[l], cache_logf_b[l], page_table)

    y_prompt, (ka_p, va_p, ki_p, kb_p, vb_p, lf_p) = trunk(
        x_prompt, jnp.arange(S, dtype=jnp.int32), a_prompt, b_prompt, w_in, b_f, w_br_a, w_br_b, w_out,
        ln1_g, ln1_b, ln2_g, ln2_b, w_gate_d, w_up_d, w_down_d, w_router, w_gate_e, w_up_e, w_down_e)
    y_sample, (ka_s, va_s, ki_s, kb_s, vb_s, lf_s) = trunk(
        x_sample, P + jnp.arange(T, dtype=jnp.int32), a_sample, b_sample, w_in, b_f, w_br_a, w_br_b, w_out,
        ln1_g, ln1_b, ln2_g, ln2_b, w_gate_d, w_up_d, w_down_d, w_router, w_gate_e, w_up_e, w_down_e)
    return (y_prompt, y_sample, ka_p, va_p, ki_p, kb_p, vb_p, lf_p, ka_s, va_s, ki_s, kb_s, vb_s, lf_s)
```

```python
import functools

import jax
import jax.numpy as jnp
import numpy as np
from jax import lax
from jax.experimental import pallas as pl
from jax.experimental.pallas import tpu as pltpu

HEAD_DIM = 64
N_HEADS_A = 8
N_KV_A = 2
N_IDX_HEADS = 4
D_IDX = 64
TOPK_MAX = 256
N_HEADS_B = 8
ROPE_THETA = 10000.0
N_EXPERTS = 8
TOP_K_EXPERTS = 2
LN_EPS = 1e-5
PAGE_SIZE = 128

WIDTH_A = N_HEADS_A * HEAD_DIM
WIDTH_B = N_HEADS_B * HEAD_DIM
KV_A = N_KV_A * HEAD_DIM
IN_SPLITS = (WIDTH_A, KV_A, KV_A, N_IDX_HEADS * D_IDX, D_IDX, N_IDX_HEADS,
             WIDTH_B, WIDTH_B, WIDTH_B, N_HEADS_B, None, None)

LANE = 128
SUB = 8
TM = 256
TQ = 128
PAGES_PER_STEP = 8
FF_CHUNK = 256
VMEM_LIMIT = 56 * 1024 * 1024

NEG = -1e30
F32 = jnp.float32
BF16 = jnp.bfloat16
I32 = jnp.int32
INT_MIN = int(np.iinfo(np.int32).min)

C_QA = 0
C_KA = C_QA + WIDTH_A
C_VA = C_KA + KV_A
C_QI = C_VA + KV_A
C_MISC = C_QI + N_IDX_HEADS * D_IDX
C_QB = C_MISC + LANE
C_KB = C_QB + WIDTH_B
C_VB = C_KB + WIDTH_B
C_END = C_VB + WIDTH_B
MISC_F = D_IDX
MISC_W = D_IDX + N_HEADS_B

_NT = (((1,), (1,)), ((), ()))


def _params(sem):
    return pltpu.CompilerParams(dimension_semantics=sem, vmem_limit_bytes=VMEM_LIMIT)


def _dot(a, b):
    return jnp.dot(a, b, preferred_element_type=F32)


def _dot_nt(a, b):
    return lax.dot_general(a, b, _NT, preferred_element_type=F32)


def _dot_exact(a, b):
    return lax.dot_general(a, b, (((1,), (0,)), ((), ())), precision=lax.Precision.HIGHEST,
                           preferred_element_type=F32)


def _layer_norm(x, g, b):
    mu = jnp.mean(x, axis=-1, keepdims=True)
    xc = x - mu
    var = jnp.mean(xc * xc, axis=-1, keepdims=True)
    return xc * lax.rsqrt(var + LN_EPS) * g + b


def _log_sigmoid(z):
    return jnp.minimum(z, 0.0) - jnp.log1p(jnp.exp(-jnp.abs(z)))


def _rope_slab(x, cos, sin_signed):
    lane = lax.broadcasted_iota(I32, x.shape, 1)
    swapped = jnp.where((lane % HEAD_DIM) < HEAD_DIM // 2,
                        pltpu.roll(x, LANE - HEAD_DIM // 2, 1), pltpu.roll(x, HEAD_DIM // 2, 1))
    return x * cos + swapped * sin_signed


def _sort_key(x):
    x = jnp.where(x == 0.0, 0.0, x)
    bits = lax.bitcast_convert_type(x, I32)
    return bits ^ ((bits >> 31) & jnp.int32(0x7FFFFFFF))


def _proj_kernel(x_ref, w_ref, cos_ref, sin_ref, bf_ref, umat_ref, ones_ref,
                 qa_ref, ka_ref, va_ref, qi_ref, kix_ref, wit_ref, lft_ref, cumt_ref,
                 qb_ref, kb_ref, vb_ref, carry_ref, *, tiles_per_seq):
    i = pl.program_id(0)
    xb = x_ref[...].astype(BF16)
    cos = cos_ref[...]
    sin = sin_ref[...]

    def proj(c0, c1):
        return _dot(xb, w_ref[:, c0:c1])

    for s in range(WIDTH_A // LANE):
        h = proj(C_QA + s * LANE, C_QA + (s + 1) * LANE)
        qa_ref[:, s * LANE:(s + 1) * LANE] = (_rope_slab(h, cos, sin) * (HEAD_DIM ** -0.5)).astype(BF16)
    ka_ref[...] = _rope_slab(proj(C_KA, C_VA), cos, sin)
    va_ref[...] = proj(C_VA, C_QI)
    for s in range(N_IDX_HEADS * D_IDX // LANE):
        h = proj(C_QI + s * LANE, C_QI + (s + 1) * LANE)
        qi_ref[:, s * LANE:(s + 1) * LANE] = _rope_slab(h, cos, sin).astype(BF16)

    misc = proj(C_MISC, C_QB)
    lane = lax.broadcasted_iota(I32, misc.shape, 1)
    kix_ref[...] = jnp.where(lane < D_IDX, _rope_slab(misc, cos, sin), misc)
    misc_t = misc.T
    wit_ref[...] = misc_t[MISC_W:MISC_W + SUB, :]
    lft = _log_sigmoid(misc_t[MISC_F:MISC_F + N_HEADS_B, :] + bf_ref[...])
    lft_ref[...] = lft

    @pl.when(i % tiles_per_seq == 0)
    def _():
        carry_ref[...] = jnp.zeros_like(carry_ref)

    carry = carry_ref[...]
    cumt_ref[...] = _dot_exact(lft, umat_ref[...]) + jnp.concatenate([carry] * (TM // LANE), axis=1)
    carry_ref[...] = carry + _dot_exact(lft, ones_ref[...])

    qb_ref[...] = (proj(C_QB, C_KB) * (HEAD_DIM ** -0.5)).astype(BF16)
    kb_ref[...] = proj(C_KB, C_VB)
    vb_ref[...] = proj(C_VB, C_END)


def _project(x, wp, cos_tab, sin_tab, bf_b, umat, ones, *, n_prompt_tiles, tiles_per_seq):
    nt, d = x.shape
    n_tiles = nt // TM

    def tab_map(i):
        return (jnp.where(i < n_prompt_tiles, i % tiles_per_seq, tiles_per_seq), 0)

    def umat_map(i):
        return (jnp.where(i < n_prompt_tiles, 0, 1), 0, 0)

    row = lambda w: pl.BlockSpec((TM, w), lambda i: (i, 0))
    col = lambda: pl.BlockSpec((SUB, TM), lambda i: (0, i))
    out_shape = (
        jax.ShapeDtypeStruct((nt, WIDTH_A), BF16), jax.ShapeDtypeStruct((nt, KV_A), F32),
        jax.ShapeDtypeStruct((nt, KV_A), F32), jax.ShapeDtypeStruct((nt, N_IDX_HEADS * D_IDX), BF16),
        jax.ShapeDtypeStruct((nt, LANE), F32), jax.ShapeDtypeStruct((SUB, nt), F32),
        jax.ShapeDtypeStruct((N_HEADS_B, nt), F32), jax.ShapeDtypeStruct((N_HEADS_B, nt), F32),
        jax.ShapeDtypeStruct((nt, WIDTH_B), BF16), jax.ShapeDtypeStruct((nt, WIDTH_B), F32),
        jax.ShapeDtypeStruct((nt, WIDTH_B), F32))
    return pl.pallas_call(
        functools.partial(_proj_kernel, tiles_per_seq=tiles_per_seq),
        grid=(n_tiles,),
        in_specs=[row(d), pl.BlockSpec((d, C_END), lambda i: (0, 0)),
                  pl.BlockSpec((TM, LANE), tab_map), pl.BlockSpec((TM, LANE), tab_map),
                  pl.BlockSpec((N_HEADS_B, TM), lambda i: (0, 0)),
                  pl.BlockSpec((None, TM, TM), umat_map), pl.BlockSpec((TM, LANE), lambda i: (0, 0))],
        out_specs=(row(WIDTH_A), row(KV_A), row(KV_A), row(N_IDX_HEADS * D_IDX), row(LANE),
                   col(), col(), col(), row(WIDTH_B), row(WIDTH_B), row(WIDTH_B)),
        out_shape=out_shape,
        scratch_shapes=[pltpu.VMEM((N_HEADS_B, LANE), F32)],
        compiler_params=_params(("arbitrary",)),
        name="project",
    )(x, wp, cos_tab, sin_tab, bf_b, umat, ones)


def _flash_head(q, k_get, v_get, bias_get, n_chunks):
    def body(c, carry):
        m, l, acc = carry
        s = _dot_nt(q, k_get(c)) + bias_get(c)
        m_new = jnp.maximum(m, jnp.max(s, axis=1, keepdims=True))
        alpha = jnp.exp(m - m_new)
        p = jnp.exp(s - m_new)
        l = alpha * l + jnp.sum(p, axis=1, keepdims=True)
        acc = alpha * acc + _dot(p.astype(BF16), v_get(c))
        return m_new, l, acc

    init = (jnp.full((TQ, 1), NEG, F32), jnp.zeros((TQ, 1), F32), jnp.zeros((TQ, LANE), F32))
    _, l, acc = lax.fori_loop(0, n_chunks, body, init)
    return acc / l


def _chunk(c):
    return pl.ds(pl.multiple_of(c * TQ, TQ), TQ)


def _dsa_prompt_kernel(qi_ref, wit_ref, kix_ref, qa_ref, ka_ref, va_ref, o_ref,
                       kip_ref, kp_ref, vp_ref, keys_ref, bias_ref, *, topk):
    i = pl.program_id(1)
    n_all = kix_ref.shape[0] // TQ
    nch = i + 1
    lane = lax.broadcasted_iota(I32, (TQ, LANE), 1)
    sub = lax.broadcasted_iota(I32, (TQ, LANE), 0)
    lo = lane < HEAD_DIM

    @pl.when(i == 0)
    def _():
        def build(c, carry):
            r = _chunk(c)
            kix = kix_ref[r, :]
            kip_ref[0, r, :] = jnp.where(lo, kix, 0.0).astype(BF16)
            kip_ref[1, r, :] = jnp.where(lo, 0.0, pltpu.roll(kix, HEAD_DIM, 1)).astype(BF16)
            for src, dst in ((ka_ref, kp_ref), (va_ref, vp_ref)):
                a = src[r, :]
                ar = pltpu.roll(a, HEAD_DIM, 1)
                dst[0, r, :] = jnp.where(lo, a, 0.0).astype(BF16)
                dst[1, r, :] = jnp.where(lo, 0.0, ar).astype(BF16)
                dst[2, r, :] = jnp.where(lo, ar, 0.0).astype(BF16)
                dst[3, r, :] = jnp.where(lo, 0.0, a).astype(BF16)
            return carry
        lax.fori_loop(0, n_all, build, 0)

    qpos = i * TQ + lane

    def score_chunk(c, carry):
        r = _chunk(c)
        acc = jnp.zeros((TQ, LANE), F32)
        for h in range(N_IDX_HEADS):
            s = _dot_nt(kip_ref[h % 2, r, :], qi_ref[:, (h // 2) * LANE:(h // 2 + 1) * LANE])
            acc = acc + wit_ref[h:h + 1, :] * jnp.maximum(s, 0.0)
        acc = jnp.where(c * TQ + sub > qpos, -jnp.inf, acc)
        keys_ref[r, :] = _sort_key(acc)
        return carry
    lax.fori_loop(0, nch, score_chunk, 0)

    def count(pred):
        def body(c, acc):
            return acc + jnp.where(pred(keys_ref[_chunk(c), :]), 1.0, 0.0)
        return jnp.sum(lax.fori_loop(0, nch, body, jnp.zeros((TQ, LANE), F32)), axis=0, keepdims=True)

    def bit_step(t, thr):
        cand = thr ^ (jnp.int32(1) << (31 - t))
        return jnp.where(count(lambda key: key >= cand) >= topk, cand, thr)
    thr = lax.fori_loop(0, 32, bit_step, jnp.full((1, LANE), INT_MIN, I32))
    need = topk - count(lambda key: key > thr)

    ltri = jnp.where(lane < sub, 1.0, 0.0).astype(BF16)

    def mask_chunk(c, n_eq):
        key = keys_ref[_chunk(c), :]
        eq = key == thr
        before = _dot(ltri, jnp.where(eq, 1.0, 0.0).astype(BF16)) + n_eq
        sel = (key > thr) | (eq & (before < need))
        sel = sel & (c * TQ + sub <= qpos)
        bias_ref[c] = jnp.where(sel, 0.0, NEG).T
        return n_eq + jnp.sum(jnp.where(eq, 1.0, 0.0), axis=0, keepdims=True)
    lax.fori_loop(0, nch, mask_chunk, jnp.zeros((1, LANE), F32))

    for j in range(N_HEADS_A // 2):
        q = qa_ref[:, j * LANE:(j + 1) * LANE]
        out = None
        for par in range(2):
            v_idx = ((2 * j + par) // (N_HEADS_A // N_KV_A)) * 2 + par
            o = _flash_head(q, lambda c: kp_ref[v_idx, _chunk(c), :], lambda c: vp_ref[v_idx, _chunk(c), :],
                            lambda c: bias_ref[c], nch)
            out = o if out is None else out + o
        o_ref[:, j * LANE:(j + 1) * LANE] = out.astype(BF16)


def _dsa_prompt(qi, wit, kix, qa, ka, va, *, batch, seq):
    topk = min(TOPK_MAX, seq // 4)
    nq = seq // TQ
    qblk = lambda w: pl.BlockSpec((TQ, w), lambda b, i: (b * nq + i, 0))
    seqblk = lambda w: pl.BlockSpec((seq, w), lambda b, i: (b, 0))
    return pl.pallas_call(
        functools.partial(_dsa_prompt_kernel, topk=float(topk)),
        grid=(batch, nq),
        in_specs=[qblk(N_IDX_HEADS * D_IDX), pl.BlockSpec((SUB, TQ), lambda b, i: (0, b * nq + i)),
                  seqblk(LANE), qblk(WIDTH_A), seqblk(KV_A), seqblk(KV_A)],
        out_specs=qblk(WIDTH_A),
        out_shape=jax.ShapeDtypeStruct((batch * seq, WIDTH_A), BF16),
        scratch_shapes=[pltpu.VMEM((2, seq, LANE), BF16), pltpu.VMEM((4, seq, LANE), BF16),
                        pltpu.VMEM((4, seq, LANE), BF16), pltpu.VMEM((seq, LANE), I32),
                        pltpu.VMEM((nq, TQ, TQ), F32)],
        compiler_params=_params(("arbitrary", "arbitrary")),
        name="dsa_prompt",
    )(qi, wit, kix, qa, ka, va)


def _fox_prompt_kernel(qb_ref, kb_ref, vb_ref, cumt_ref, o_ref, kp_ref, vp_ref, negc_ref):
    i = pl.program_id(1)
    n_all = kb_ref.shape[0] // TQ
    lane = lax.broadcasted_iota(I32, (TQ, LANE), 1)
    sub = lax.broadcasted_iota(I32, (TQ, LANE), 0)
    lo = lane < HEAD_DIM

    @pl.when(i == 0)
    def _():
        for c in range(n_all):
            negc_ref[c] = -cumt_ref[:, c * TQ:(c + 1) * TQ]

        def build(c, carry):
            r = _chunk(c)
            for src, dst in ((kb_ref, kp_ref), (vb_ref, vp_ref)):
                for j in range(N_HEADS_B // 2):
                    a = src[r, j * LANE:(j + 1) * LANE]
                    dst[2 * j, r, :] = jnp.where(lo, a, 0.0).astype(BF16)
                    dst[2 * j + 1, r, :] = jnp.where(lo, 0.0, a).astype(BF16)
            return carry
        lax.fori_loop(0, n_all, build, 0)

    for j in range(N_HEADS_B // 2):
        q = qb_ref[:, j * LANE:(j + 1) * LANE]
        out = None
        for par in range(2):
            h = 2 * j + par

            def bias(c, h=h):
                causal = jnp.where(c * TQ + lane > i * TQ + sub, NEG, 0.0)
                return causal + negc_ref[c][h:h + 1, :]
            o = _flash_head(q, lambda c: kp_ref[h, _chunk(c), :], lambda c: vp_ref[h, _chunk(c), :], bias, i + 1)
            out = o if out is None else out + o
        o_ref[:, j * LANE:(j + 1) * LANE] = out.astype(BF16)


def _fox_prompt(qb, kb, vb, cumt, *, batch, seq):
    nq = seq // TQ
    qblk = pl.BlockSpec((TQ, WIDTH_B), lambda b, i: (b * nq + i, 0))
    seqblk = pl.BlockSpec((seq, WIDTH_B), lambda b, i: (b, 0))
    return pl.pallas_call(
        _fox_prompt_kernel,
        grid=(batch, nq),
        in_specs=[qblk, seqblk, seqblk, pl.BlockSpec((N_HEADS_B, seq), lambda b, i: (0, b))],
        out_specs=qblk,
        out_shape=jax.ShapeDtypeStruct((batch * seq, WIDTH_B), BF16),
        scratch_shapes=[pltpu.VMEM((N_HEADS_B, seq, LANE), BF16), pltpu.VMEM((N_HEADS_B, seq, LANE), BF16),
                        pltpu.VMEM((nq, N_HEADS_B, TQ), F32)],
        compiler_params=_params(("arbitrary", "arbitrary")),
        name="fox_prompt",
    )(qb, kb, vb, cumt)


def _rows_to_heads(x, n_heads):
    return jnp.concatenate([x] * n_heads, axis=0)


def _softmax_step(qm, ks, vs, biases, m_ref, l_ref, acc_ref):
    logits = [_dot_nt(qm, k) + b for k, b in zip(ks, biases)]
    smax = functools.reduce(jnp.maximum, logits)
    m_old = m_ref[...]
    m_new = jnp.maximum(m_old, jnp.max(smax, axis=1, keepdims=True))
    alpha = jnp.exp(m_old - m_new)
    ps = [jnp.exp(s - m_new) for s in logits]
    l_ref[...] = alpha * l_ref[...] + jnp.sum(functools.reduce(jnp.add, ps), axis=1, keepdims=True)
    pv = functools.reduce(jnp.add, [_dot(p.astype(BF16), v) for p, v in zip(ps, vs)])
    acc_ref[...] = alpha * acc_ref[...] + pv
    m_ref[...] = m_new


def _pad_rows(x):
    return jnp.concatenate([x, jnp.zeros((TQ - x.shape[0], x.shape[1]), x.dtype)], axis=0)


def _new_key_bias(n_rows, n_new):
    lane = lax.broadcasted_iota(I32, (n_rows, LANE), 1)
    t = lax.broadcasted_iota(I32, (n_rows, LANE), 0) % n_new
    return jnp.where((lane <= t) & (lane < n_new), 0.0, NEG)


def _dsa_select_kernel(pt_ref, qi_ref, kix_ref, *rest, topk, n_pages):
    g_pages = PAGES_PER_STEP
    page_refs = rest[:g_pages]
    o_ref = rest[g_pages]
    qim_ref, w_ref, keys_ref = rest[g_pages + 1:]
    j = pl.program_id(1)
    nj = pl.num_programs(1)
    n_q = qi_ref.shape[0]
    nch = n_pages + 1

    @pl.when(j == 0)
    def _():
        pieces, ws = [], []
        kix = kix_ref[...]
        for h in range(N_IDX_HEADS):
            slab = qi_ref[:, (h // 2) * LANE:(h // 2 + 1) * LANE].astype(F32)
            if h % 2:
                slab = pltpu.roll(slab, HEAD_DIM, 1)
            pieces.append(slab[:, :D_IDX])
            ws.append(jnp.broadcast_to(kix[:, MISC_W + h:MISC_W + h + 1], (n_q, LANE)))
        qim_ref[...] = jnp.concatenate(pieces, axis=0).astype(BF16)
        w_ref[...] = jnp.concatenate(ws, axis=0)

    def scores(keys_bf16):
        r = jnp.maximum(_dot_nt(qim_ref[...], keys_bf16), 0.0) * w_ref[...]
        return functools.reduce(jnp.add, [r[h * n_q:(h + 1) * n_q] for h in range(N_IDX_HEADS)])

    for g in range(g_pages):
        keys_ref[j * g_pages + g] = _sort_key(scores(page_refs[g][...].astype(BF16)))

    @pl.when(j == nj - 1)
    def _():
        new = scores(_pad_rows(kix_ref[:, :D_IDX]).astype(BF16))
        vis = _new_key_bias(n_q, n_q) == 0.0
        keys_ref[n_pages] = _sort_key(jnp.where(vis, new, -jnp.inf))

        def count(pred):
            def body(c, acc):
                return acc + jnp.where(pred(keys_ref[c]), 1.0, 0.0)
            return jnp.sum(lax.fori_loop(0, nch, body, jnp.zeros((n_q, LANE), F32)), axis=1, keepdims=True)

        def bit_step(t, thr):
            cand = thr ^ (jnp.int32(1) << (31 - t))
            return jnp.where(count(lambda key: key >= cand) >= topk, cand, thr)
        thr = lax.fori_loop(0, 32, bit_step, jnp.full((n_q, 1), INT_MIN, I32))
        need = topk - count(lambda key: key > thr)

        lane = lax.broadcasted_iota(I32, (LANE, LANE), 1)
        sub = lax.broadcasted_iota(I32, (LANE, LANE), 0)
        utri = jnp.where(sub < lane, 1.0, 0.0).astype(BF16)

        def mask_chunk(c, n_eq):
            key = keys_ref[c]
            eq = key == thr
            eqf = jnp.where(eq, 1.0, 0.0)
            before = _dot(eqf.astype(BF16), utri) + n_eq
            sel = (key > thr) | (eq & (before < need))
            sel = sel & ((c < n_pages) | vis)
            o_ref[c] = jnp.where(sel, 0.0, NEG)
            return n_eq + jnp.sum(eqf, axis=1, keepdims=True)
        lax.fori_loop(0, nch, mask_chunk, jnp.zeros((n_q, 1), F32))


def _dsa_select(page_table, qi, kix, pool_kidx, layer, *, n_prompt, dec_batch, n_new):
    n_pages = page_table.shape[1]
    g = PAGES_PER_STEP
    nj = n_pages // g
    topk = min(TOPK_MAX, (n_pages * PAGE_SIZE + n_new) // 4)
    base = n_prompt // n_new
    row = lambda w: pl.BlockSpec((n_new, w), lambda b, j, pt: (base + b, 0))

    def page_spec(gi):
        return pl.BlockSpec((None, None, PAGE_SIZE, D_IDX), lambda b, j, pt: (layer, pt[b, j * g + gi], 0, 0))

    grid_spec = pltpu.PrefetchScalarGridSpec(
        num_scalar_prefetch=1, grid=(dec_batch, nj),
        in_specs=[row(N_IDX_HEADS * D_IDX), row(LANE)] + [page_spec(gi) for gi in range(g)],
        out_specs=pl.BlockSpec((None, n_pages + 1, n_new, LANE), lambda b, j, pt: (b, 0, 0, 0)),
        scratch_shapes=[pltpu.VMEM((N_IDX_HEADS * n_new, D_IDX), BF16), pltpu.VMEM((N_IDX_HEADS * n_new, LANE), F32),
                        pltpu.VMEM((n_pages + 1, n_new, LANE), I32)])
    return pl.pallas_call(
        functools.partial(_dsa_select_kernel, topk=float(topk), n_pages=n_pages),
        grid_spec=grid_spec,
        out_shape=jax.ShapeDtypeStruct((dec_batch, n_pages + 1, n_new, LANE), F32),
        compiler_params=_params(("arbitrary", "arbitrary")),
        name="dsa_select",
    )(page_table, qi, kix, *([pool_kidx] * g))


def _dsa_attend_kernel(pt_ref, qa_ref, knew_ref, vnew_ref, sel_ref, selnew_ref, *rest):
    g_pages = PAGES_PER_STEP
    k_refs = rest[:g_pages]
    v_refs = rest[g_pages:2 * g_pages]
    o_ref = rest[2 * g_pages]
    qm_ref, m_ref, l_ref, acc_ref = rest[2 * g_pages + 1:]
    j = pl.program_id(1)
    nj = pl.num_programs(1)
    n_q = qa_ref.shape[0]
    heads_per_kv = N_HEADS_A // N_KV_A
    lane = lax.broadcasted_iota(I32, (n_q, LANE), 1)
    lo = lane < HEAD_DIM

    @pl.when(j == 0)
    def _():
        pieces = []
        for h in range(N_HEADS_A):
            slab = qa_ref[:, (h // 2) * LANE:(h // 2 + 1) * LANE].astype(F32)
            kv = h // heads_per_kv
            if h % 2 != kv:
                slab = pltpu.roll(slab, HEAD_DIM, 1)
            pieces.append(jnp.where(lo == (kv == 0), slab, 0.0))
        qm_ref[...] = jnp.concatenate(pieces, axis=0).astype(BF16)
        m_ref[...] = jnp.full(m_ref.shape, NEG, F32)
        l_ref[...] = jnp.zeros_like(l_ref)
        acc_ref[...] = jnp.zeros_like(acc_ref)

    qm = qm_ref[...]
    _softmax_step(qm, [k[...].astype(BF16) for k in k_refs], [v[...].astype(BF16) for v in v_refs],
                  [_rows_to_heads(sel_ref[g], N_HEADS_A) for g in range(g_pages)], m_ref, l_ref, acc_ref)

    @pl.when(j == nj - 1)
    def _():
        _softmax_step(qm, [_pad_rows(knew_ref[...]).astype(BF16)], [_pad_rows(vnew_ref[...]).astype(BF16)],
                      [_rows_to_heads(selnew_ref[...], N_HEADS_A)], m_ref, l_ref, acc_ref)
        out = acc_ref[...] / l_ref[...]
        for s in range(N_HEADS_A // 2):
            a0 = out[(2 * s) * n_q:(2 * s + 1) * n_q]
            a1 = out[(2 * s + 1) * n_q:(2 * s + 2) * n_q]
            if (2 * s) // heads_per_kv == 0:
                slab = jnp.where(lo, a0, pltpu.roll(a1, HEAD_DIM, 1))
            else:
                slab = jnp.where(lo, pltpu.roll(a0, HEAD_DIM, 1), a1)
            o_ref[:, s * LANE:(s + 1) * LANE] = slab.astype(BF16)


def _dsa_attend(page_table, qa, ka, va, sel, pool_k, pool_v, layer, *, n_prompt, dec_batch, n_new):
    n_pages = page_table.shape[1]
    g = PAGES_PER_STEP
    nj = n_pages // g
    base = n_prompt // n_new
    row = lambda w: pl.BlockSpec((n_new, w), lambda b, j, pt: (base + b, 0))

    def page_spec(gi):
        return pl.BlockSpec((None, None, PAGE_SIZE, KV_A), lambda b, j, pt: (layer, pt[b, j * g + gi], 0, 0))

    grid_spec = pltpu.PrefetchScalarGridSpec(
        num_scalar_prefetch=1, grid=(dec_batch, nj),
        in_specs=[row(WIDTH_A), row(KV_A), row(KV_A),
                  pl.BlockSpec((None, g, n_new, LANE), lambda b, j, pt: (b, j, 0, 0)),
                  pl.BlockSpec((None, None, n_new, LANE), lambda b, j, pt: (b, n_pages, 0, 0))]
        + [page_spec(gi) for gi in range(g)] * 2,
        out_specs=pl.BlockSpec((n_new, WIDTH_A), lambda b, j, pt: (b, 0)),
        scratch_shapes=[pltpu.VMEM((N_HEADS_A * n_new, KV_A), BF16), pltpu.VMEM((N_HEADS_A * n_new, 1), F32),
                        pltpu.VMEM((N_HEADS_A * n_new, 1), F32), pltpu.VMEM((N_HEADS_A * n_new, KV_A), F32)])
    return pl.pallas_call(
        _dsa_attend_kernel, grid_spec=grid_spec,
        out_shape=jax.ShapeDtypeStruct((dec_batch * n_new, WIDTH_A), BF16),
        compiler_params=_params(("arbitrary", "arbitrary")),
        name="dsa_attend",
    )(page_table, qa, ka, va, sel, sel, *([pool_k] * g), *([pool_v] * g))


def _fox_sample_kernel(pt_ref, qb_ref, knew_ref, vnew_ref, lfnew_ref, *rest):
    g_pages = PAGES_PER_STEP
    k_refs = rest[:g_pages]
    v_refs = rest[g_pages:2 * g_pages]
    lf_refs = rest[2 * g_pages:3 * g_pages]
    o_ref = rest[3 * g_pages]
    qm_ref, m_ref, l_ref, acc_ref, carry_ref = rest[3 * g_pages + 1:]
    j = pl.program_id(1)
    nj = pl.num_programs(1)
    n_q = qb_ref.shape[0]
    n_rows = N_HEADS_B * n_q

    @pl.when(j == 0)
    def _():
        q = jnp.concatenate([qb_ref[...]] * N_HEADS_B, axis=0)
        head_of_lane = lax.broadcasted_iota(I32, q.shape, 1) // HEAD_DIM
        head_of_row = lax.broadcasted_iota(I32, q.shape, 0) // n_q
        qm_ref[...] = jnp.where(head_of_lane == head_of_row, q, jnp.zeros_like(q))
        m_ref[...] = jnp.full(m_ref.shape, NEG, F32)
        l_ref[...] = jnp.zeros_like(l_ref)
        acc_ref[...] = jnp.zeros_like(acc_ref)
        carry_ref[...] = jnp.zeros_like(carry_ref)

    lane = lax.broadcasted_iota(I32, (LANE, LANE), 1)
    sub = lax.broadcasted_iota(I32, (LANE, LANE), 0)
    u_incl = jnp.where(sub <= lane, 1.0, 0.0)
    ones = jnp.ones((LANE, LANE), F32)

    def bias_chunks(lfs):
        out = []
        carry = carry_ref[...]
        for lf in lfs:
            rows = jnp.concatenate([jnp.broadcast_to(lf[h:h + 1, :], (n_q, LANE)) for h in range(N_HEADS_B)], axis=0)
            out.append(-(carry + _dot_exact(rows, u_incl)))
            carry = carry + _dot_exact(rows, ones)
        carry_ref[...] = carry
        return out

    qm = qm_ref[...]
    _softmax_step(qm, [k[...].astype(BF16) for k in k_refs], [v[...].astype(BF16) for v in v_refs],
                  bias_chunks([r[...] for r in lf_refs]), m_ref, l_ref, acc_ref)

    @pl.when(j == nj - 1)
    def _():
        bias = bias_chunks([lfnew_ref[...]])[0] + _new_key_bias(n_rows, n_q)
        _softmax_step(qm, [_pad_rows(knew_ref[...]).astype(BF16)], [_pad_rows(vnew_ref[...]).astype(BF16)],
                      [bias], m_ref, l_ref, acc_ref)
        out = acc_ref[...] / l_ref[...]
        head_of_lane = lax.broadcasted_iota(I32, (n_q, WIDTH_B), 1) // HEAD_DIM
        res = jnp.zeros((n_q, WIDTH_B), F32)
        for h in range(N_HEADS_B):
            res = res + jnp.where(head_of_lane == h, out[h * n_q:(h + 1) * n_q], 0.0)
        o_ref[...] = res.astype(BF16)


def _fox_sample(page_table, qb, kb, vb, lf_new, pool_k, pool_v, pool_lft, layer, *, n_prompt, dec_batch, n_new):
    n_pages = page_table.shape[1]
    g = PAGES_PER_STEP
    nj = n_pages // g
    base = n_prompt // n_new
    row = lambda w: pl.BlockSpec((n_new, w), lambda b, j, pt: (base + b, 0))

    def page_spec(gi, rows, w):
        return pl.BlockSpec((None, None, rows, w), lambda b, j, pt: (layer, pt[b, j * g + gi], 0, 0))

    n_rows = N_HEADS_B * n_new
    grid_spec = pltpu.PrefetchScalarGridSpec(
        num_scalar_prefetch=1, grid=(dec_batch, nj),
        in_specs=[row(WIDTH_B), row(WIDTH_B), row(WIDTH_B),
                  pl.BlockSpec((None, N_HEADS_B, LANE), lambda b, j, pt: (b, 0, 0))]
        + [page_spec(gi, PAGE_SIZE, WIDTH_B) for gi in range(g)] * 2
        + [page_spec(gi, N_HEADS_B, PAGE_SIZE) for gi in range(g)],
        out_specs=pl.BlockSpec((n_new, WIDTH_B), lambda b, j, pt: (b, 0)),
        scratch_shapes=[pltpu.VMEM((n_rows, WIDTH_B), BF16), pltpu.VMEM((n_rows, 1), F32),
                        pltpu.VMEM((n_rows, 1), F32), pltpu.VMEM((n_rows, WIDTH_B), F32),
                        pltpu.VMEM((n_rows, LANE), F32)])
    return pl.pallas_call(
        _fox_sample_kernel, grid_spec=grid_spec,
        out_shape=jax.ShapeDtypeStruct((dec_batch * n_new, WIDTH_B), BF16),
        compiler_params=_params(("arbitrary", "arbitrary")),
        name="fox_sample",
    )(page_table, qb, kb, vb, lf_new, *([pool_k] * g), *([pool_v] * g), *([pool_lft] * g))


def _merge_kernel(x_ref, oa_ref, ob_ref, wg_ref, wa_ref, wb_ref, wo_ref, g_ref, b_ref, *rest, alpha, d_model):
    x = x_ref[...]
    xb = x.astype(BF16)
    ua = _dot(oa_ref[...], wa_ref[...])
    ub = _dot(ob_ref[...], wb_ref[...])
    merged = jax.nn.sigmoid(_dot(xb, wg_ref[:, :d_model])) * ua + jax.nn.sigmoid(_dot(xb, wg_ref[:, d_model:])) * ub
    mix = _dot(merged.astype(BF16), wo_ref[...])
    y = _layer_norm(alpha * x + mix, g_ref[...], b_ref[...])
    if len(rest) == 1:
        rest[0][...] = y
    else:
        wr_ref, y_ref, yb_ref, lg_ref = rest
        y_ref[...] = y
        yb = y.astype(BF16)
        yb_ref[...] = yb
        lg_ref[...] = _dot(yb, wr_ref[...])


def _merge(x, oa, ob, wg, wa, wb, wo, g, b, wr, *, alpha):
    nt, d = x.shape
    row = lambda w: pl.BlockSpec((TM, w), lambda i: (i, 0))
    full = lambda a: pl.BlockSpec(a.shape, lambda i: (0,) * a.ndim)
    ins = [x, oa, ob, wg, wa, wb, wo, g, b]
    in_specs = [row(d), row(WIDTH_A), row(WIDTH_B)] + [full(a) for a in ins[3:]]
    if wr is None:
        out_shape = jax.ShapeDtypeStruct((nt, d), F32)
        out_specs = row(d)
    else:
        ins.append(wr)
        in_specs.append(full(wr))
        out_shape = (jax.ShapeDtypeStruct((nt, d), F32), jax.ShapeDtypeStruct((nt, d), BF16),
                     jax.ShapeDtypeStruct((nt, LANE), F32))
        out_specs = (row(d), row(d), row(LANE))
    return pl.pallas_call(
        functools.partial(_merge_kernel, alpha=alpha, d_model=d),
        grid=(nt // TM,), in_specs=in_specs, out_specs=out_specs, out_shape=out_shape,
        compiler_params=_params(("arbitrary",)),
        name="merge",
    )(*ins)


def _swiglu(xb, wg_ref, wu_ref, wd_ref):
    d_ff = wg_ref.shape[-1]
    acc = None
    for c in range(0, d_ff, FF_CHUNK):
        h = jax.nn.silu(_dot(xb, wg_ref[:, c:c + FF_CHUNK])) * _dot(xb, wu_ref[:, c:c + FF_CHUNK])
        part = _dot(h.astype(BF16), wd_ref[c:c + FF_CHUNK, :])
        acc = part if acc is None else acc + part
    return acc


def _ffn_kernel(x_ref, wg_ref, wu_ref, wd_ref, g_ref, b_ref, y_ref, *, alpha):
    x = x_ref[...]
    f = _swiglu(x.astype(BF16), wg_ref, wu_ref, wd_ref)
    y_ref[...] = _layer_norm(alpha * x + f, g_ref[...], b_ref[...])


def _ffn(x, wg, wu, wd, g, b, *, alpha):
    nt, d = x.shape
    row = pl.BlockSpec((TM, d), lambda i: (i, 0))
    full = lambda a: pl.BlockSpec(a.shape, lambda i: (0,) * a.ndim, pipeline_mode=pl.Buffered(1))
    return pl.pallas_call(
        functools.partial(_ffn_kernel, alpha=alpha),
        grid=(nt // TM,), in_specs=[row, full(wg), full(wu), full(wd), full(g), full(b)], out_specs=row,
        out_shape=jax.ShapeDtypeStruct((nt, d), F32),
        compiler_params=_params(("arbitrary",)),
        name="ffn",
    )(x, wg, wu, wd, g, b)


def _moe_kernel(te_ref, nu_ref, xs_ref, gate_ref, wg_ref, wu_ref, wd_ref, o_ref):
    @pl.when(pl.program_id(0) < nu_ref[0])
    def _():
        o_ref[...] = gate_ref[...] * _swiglu(xs_ref[...], wg_ref, wu_ref, wd_ref)


def _moe_experts(tile_expert, n_used, xs, gates, wg, wu, wd):
    rows, d = xs.shape
    d_ff = wg.shape[-1]
    grid_spec = pltpu.PrefetchScalarGridSpec(
        num_scalar_prefetch=2, grid=(rows // TM,),
        in_specs=[pl.BlockSpec((TM, d), lambda i, te, nu: (i, 0)), pl.BlockSpec((TM, 1), lambda i, te, nu: (i, 0)),
                  pl.BlockSpec((None, d, d_ff), lambda i, te, nu: (te[i], 0, 0)),
                  pl.BlockSpec((None, d, d_ff), lambda i, te, nu: (te[i], 0, 0)),
                  pl.BlockSpec((None, d_ff, d), lambda i, te, nu: (te[i], 0, 0))],
        out_specs=pl.BlockSpec((TM, d), lambda i, te, nu: (i, 0)))
    return pl.pallas_call(
        _moe_kernel, grid_spec=grid_spec, out_shape=jax.ShapeDtypeStruct((rows, d), F32),
        compiler_params=_params(("arbitrary",)),
        name="moe_experts",
    )(tile_expert, n_used, xs, gates, wg, wu, wd)


def _combine_kernel(x_ref, o1_ref, o2_ref, g_ref, b_ref, y_ref, *, alpha):
    y_ref[...] = _layer_norm(alpha * x_ref[...] + (o1_ref[...] + o2_ref[...]), g_ref[...], b_ref[...])


def _combine(x, o1, o2, g, b, *, alpha):
    nt, d = x.shape
    row = pl.BlockSpec((TM, d), lambda i: (i, 0))
    full = lambda a: pl.BlockSpec(a.shape, lambda i: (0,) * a.ndim)
    return pl.pallas_call(
        functools.partial(_combine_kernel, alpha=alpha),
        grid=(nt // TM,), in_specs=[row, row, row, full(g), full(b)], out_specs=row,
        out_shape=jax.ShapeDtypeStruct((nt, d), F32),
        compiler_params=_params(("arbitrary",)),
        name="moe_combine",
    )(x, o1, o2, g, b)


def _route(logits, n_rows):
    nt = logits.shape[0]
    top_v, top_i = lax.top_k(logits[:, :N_EXPERTS], TOP_K_EXPERTS)
    top_w = jax.nn.softmax(top_v, axis=-1)
    e_flat = top_i.reshape(-1)
    onehot = (e_flat[:, None] == jnp.arange(N_EXPERTS, dtype=I32)[None, :]).astype(I32)
    rank = jnp.sum((jnp.cumsum(onehot, axis=0) - onehot) * onehot, axis=1)
    counts = jnp.sum(onehot, axis=0)
    padded = ((counts + TM - 1) // TM) * TM
    ends = jnp.cumsum(padded)
    dest = (ends - padded)[e_flat] + rank
    token = jnp.arange(nt * TOP_K_EXPERTS, dtype=I32) // TOP_K_EXPERTS
    row_token = jnp.zeros((n_rows,), I32).at[dest].set(token)
    row_gate = jnp.zeros((n_rows,), F32).at[dest].set(top_w.reshape(-1))
    tile_start = jnp.arange(n_rows // TM, dtype=I32) * TM
    tile_expert = jnp.minimum(jnp.searchsorted(ends, tile_start, side="right"), N_EXPERTS - 1).astype(I32)
    n_used = (ends[-1] // TM).astype(I32).reshape(1)
    return dest.reshape(nt, TOP_K_EXPERTS), row_token, row_gate.reshape(n_rows, 1), tile_expert, n_used


def kernel(x_prompt, x_sample, cache_k_a, cache_v_a, cache_kidx_a, cache_k_b, cache_v_b, cache_logf_b,
           page_table, w_in, b_f, w_br_a, w_br_b, w_out, ln1_g, ln1_b, ln2_g, ln2_b,
           w_gate_d, w_up_d, w_down_d, w_router, w_gate_e, w_up_e, w_down_e):
    batch, seq, d = x_prompt.shape
    dec_batch, n_new, _ = x_sample.shape
    depth = w_in.shape[0]
    n_pool = cache_k_a.shape[1]
    n_pages = page_table.shape[1]
    past = n_pages * PAGE_SIZE
    n_prompt = batch * seq
    n_sample = dec_batch * n_new
    nt = n_prompt + n_sample
    assert seq % TM == 0 and n_sample == TM and n_pages % PAGES_PER_STEP == 0 and n_new == SUB
    alpha = (2.0 * depth) ** 0.25
    tiles_per_seq = seq // TM

    splits = [s if s is not None else d for s in IN_SPLITS]
    offs = np.cumsum([0] + splits)
    col = lambda k: w_in[:, :, offs[k]:offs[k + 1]]
    pad = jnp.zeros((depth, d, LANE - D_IDX - N_HEADS_B - N_IDX_HEADS), w_in.dtype)
    wp = jnp.concatenate([col(0), col(1), col(2), col(3), col(4), col(9), col(5), pad, col(6), col(7), col(8)],
                         axis=-1).astype(BF16)
    wgates = w_in[:, :, offs[10]:].astype(BF16)
    wa, wb, wo = w_br_a.astype(BF16), w_br_b.astype(BF16), w_out.astype(BF16)
    wgd, wud, wdd = w_gate_d.astype(BF16), w_up_d.astype(BF16), w_down_d.astype(BF16)
    wge, wue, wde = w_gate_e.astype(BF16), w_up_e.astype(BF16), w_down_e.astype(BF16)
    wr = jnp.pad(w_router, ((0, 0), (0, 0), (0, LANE - N_EXPERTS))).astype(BF16)
    row2 = lambda a: a.reshape(depth, 1, d)
    g1, b1, g2, b2 = row2(ln1_g), row2(ln1_b), row2(ln2_g), row2(ln2_b)
    bf_b = jnp.broadcast_to(b_f[:, :, None], (depth, N_HEADS_B, TM))

    inv = ROPE_THETA ** (-jnp.arange(0, HEAD_DIM, 2, dtype=F32) / HEAD_DIM)
    pos = jnp.concatenate([jnp.arange(seq, dtype=I32), past + jnp.arange(TM, dtype=I32) % n_new]).astype(F32)
    ang = pos[:, None] * inv[None, :]
    cos_tab = jnp.tile(jnp.cos(ang), (1, LANE // (HEAD_DIM // 2)))
    sin_tab = jnp.tile(jnp.concatenate([-jnp.sin(ang), jnp.sin(ang)], axis=1), (1, LANE // HEAD_DIM))
    r = jnp.arange(TM)
    upper = r[:, None] <= r[None, :]
    umat = jnp.stack([upper, upper & (r[:, None] // n_new == r[None, :] // n_new)]).astype(F32)
    ones = jnp.ones((TM, LANE), F32)

    pool_ka = cache_k_a.reshape(depth, n_pool, PAGE_SIZE, KV_A)
    pool_va = cache_v_a.reshape(depth, n_pool, PAGE_SIZE, KV_A)
    pool_kb = cache_k_b.reshape(depth, n_pool, PAGE_SIZE, WIDTH_B)
    pool_vb = cache_v_b.reshape(depth, n_pool, PAGE_SIZE, WIDTH_B)
    pool_lft = jnp.swapaxes(cache_logf_b, 2, 3)

    n_moe_rows = ((nt * TOP_K_EXPERTS + N_EXPERTS * (TM - 1)) // TM + 1) * TM

    x = jnp.concatenate([x_prompt.reshape(n_prompt, d), x_sample.reshape(n_sample, d)], axis=0)
    rows = []
    for l in range(depth):
        qa, ka, va, qi, kix, wit, lft, cumt, qb, kb, vb = _project(
            x, wp[l], cos_tab, sin_tab, bf_b[l], umat, ones, n_prompt_tiles=n_prompt // TM, tiles_per_seq=tiles_per_seq)

        oa_p = _dsa_prompt(qi, wit, kix, qa, ka, va, batch=batch, seq=seq)
        ob_p = _fox_prompt(qb, kb, vb, cumt, batch=batch, seq=seq)

        skw = dict(n_prompt=n_prompt, dec_batch=dec_batch, n_new=n_new)
        sel = _dsa_select(page_table, qi, kix, cache_kidx_a, l, **skw)
        oa_s = _dsa_attend(page_table, qa, ka, va, sel, pool_ka, pool_va, l, **skw)
        lf_new = lft[:, n_prompt:].reshape(N_HEADS_B, dec_batch, n_new).transpose(1, 0, 2)
        lf_new = jnp.pad(lf_new, ((0, 0), (0, 0), (0, LANE - n_new)))
        ob_s = _fox_sample(page_table, qb, kb, vb, lf_new, pool_kb, pool_vb, pool_lft, l, **skw)

        oa = jnp.concatenate([oa_p, oa_s], axis=0)
        ob = jnp.concatenate([ob_p, ob_s], axis=0)
        if l % 2 == 0:
            x1 = _merge(x, oa, ob, wgates[l], wa[l], wb[l], wo[l], g1[l], b1[l], None, alpha=alpha)
            x = _ffn(x1, wgd[l // 2], wud[l // 2], wdd[l // 2], g2[l], b2[l], alpha=alpha)
        else:
            x1, x1b, logits = _merge(x, oa, ob, wgates[l], wa[l], wb[l], wo[l], g1[l], b1[l], wr[l // 2], alpha=alpha)
            dest, row_token, row_gate, tile_expert, n_used = _route(logits, n_moe_rows)
            ys = _moe_experts(tile_expert, n_used, x1b[row_token], row_gate, wge[l // 2], wue[l // 2], wde[l // 2])
            x = _combine(x1, ys[dest[:, 0]], ys[dest[:, 1]], g2[l], b2[l], alpha=alpha)
        rows.append((ka, va, kix[:, :D_IDX], kb, vb, lft.T))

    def split(k, tail):
        full = jnp.stack([r[k] for r in rows])
        return (full[:, :n_prompt].reshape((depth, batch, seq) + tail),
                full[:, n_prompt:].reshape((depth, dec_batch, n_new) + tail))

    tails = ((N_KV_A, HEAD_DIM), (N_KV_A, HEAD_DIM), (D_IDX,), (N_HEADS_B, HEAD_DIM), (N_HEADS_B, HEAD_DIM), (N_HEADS_B,))
    outs = [split(k, t) for k, t in enumerate(tails)]
    return ((x[:n_prompt].reshape(batch, seq, d), x[n_prompt:].reshape(dec_batch, n_new, d))
            + tuple(o[0] for o in outs) + tuple(o[1] for o in outs))
```

```python
import functools

import jax
import jax.numpy as jnp
import numpy as np
from jax import lax
from jax.experimental import pallas as pl
from jax.experimental.pallas import tpu as pltpu

HEAD_DIM = 64
N_HEADS_A = 8
N_KV_A = 2
N_IDX_HEADS = 4
D_IDX = 64
TOPK_MAX = 256
N_HEADS_B = 8
ROPE_THETA = 10000.0
N_EXPERTS = 8
TOP_K_EXPERTS = 2
LN_EPS = 1e-5
PAGE_SIZE = 128

WIDTH_A = N_HEADS_A * HEAD_DIM
WIDTH_B = N_HEADS_B * HEAD_DIM
KV_A = N_KV_A * HEAD_DIM
IN_SPLITS = (WIDTH_A, KV_A, KV_A, N_IDX_HEADS * D_IDX, D_IDX, N_IDX_HEADS,
             WIDTH_B, WIDTH_B, WIDTH_B, N_HEADS_B, None, None)

LANE = 128
SUB = 8
TM = 256
TQ = 128
PAGES_PER_STEP = 8
FF_CHUNK = 256
KEY_BLOCK = 512
VMEM_LIMIT = 56 * 1024 * 1024

NEG = -1e30
F32 = jnp.float32
BF16 = jnp.bfloat16
I32 = jnp.int32
INT_MIN = int(np.iinfo(np.int32).min)
KEY_NEG_INF = INT_MIN + 0x7FFFFF

C_QA = 0
C_KA = C_QA + WIDTH_A
C_VA = C_KA + KV_A
C_QI = C_VA + KV_A
C_MISC = C_QI + N_IDX_HEADS * D_IDX
C_QB = C_MISC + LANE
C_KB = C_QB + WIDTH_B
C_VB = C_KB + WIDTH_B
C_END = C_VB + WIDTH_B
MISC_F = D_IDX
MISC_W = D_IDX + N_HEADS_B

_NT = (((1,), (1,)), ((), ()))


def _params(sem):
    return pltpu.CompilerParams(dimension_semantics=sem, vmem_limit_bytes=VMEM_LIMIT)


def _dot(a, b):
    return jnp.dot(a, b, preferred_element_type=F32)


def _dot_nt(a, b):
    return lax.dot_general(a, b, _NT, preferred_element_type=F32)


def _dot_exact(a, b):
    return lax.dot_general(a, b, (((1,), (0,)), ((), ())), precision=lax.Precision.HIGHEST,
                           preferred_element_type=F32)


def _layer_norm(x, g, b):
    mu = jnp.mean(x, axis=-1, keepdims=True)
    xc = x - mu
    var = jnp.mean(xc * xc, axis=-1, keepdims=True)
    return xc * lax.rsqrt(var + LN_EPS) * g + b


def _log_sigmoid(z):
    return jnp.minimum(z, 0.0) - jnp.log1p(jnp.exp(-jnp.abs(z)))


def _rope_slab(x, cos, sin_signed):
    lane = lax.broadcasted_iota(I32, x.shape, 1)
    swapped = jnp.where((lane % HEAD_DIM) < HEAD_DIM // 2,
                        pltpu.roll(x, LANE - HEAD_DIM // 2, 1), pltpu.roll(x, HEAD_DIM // 2, 1))
    return x * cos + swapped * sin_signed


def _split3(x):
    c1 = x.astype(BF16).astype(F32)
    r1 = x - c1
    c2 = r1.astype(BF16).astype(F32)
    return c1, c2, (r1 - c2).astype(BF16).astype(F32)


def _proj_kernel(x_ref, w_ref, cos_ref, sin_ref, bf_ref, umat_ref, ones_ref,
                 qa_ref, qi_ref, qb_ref, kix_ref, wit_ref, lft_ref, kab_ref, kbb_ref, cb_ref,
                 kat_ref, vat_ref, kit_ref, kbt_ref, vbt_ref, lf3_ref,
                 kas_ref, vas_ref, kbs_ref, vbs_ref, carry_ref, *, tiles_per_seq, n_prompt_tiles):
    i = pl.program_id(0)
    xb = x_ref[...].astype(BF16)
    cos = cos_ref[...]
    sin = sin_ref[...]

    def proj(c0, c1):
        return _dot(xb, w_ref[:, c0:c1])

    for s in range(WIDTH_A // LANE):
        h = proj(C_QA + s * LANE, C_QA + (s + 1) * LANE)
        qa_ref[:, s * LANE:(s + 1) * LANE] = (_rope_slab(h, cos, sin) * (HEAD_DIM ** -0.5)).astype(BF16)
    for s in range(N_IDX_HEADS * D_IDX // LANE):
        h = proj(C_QI + s * LANE, C_QI + (s + 1) * LANE)
        qi_ref[:, s * LANE:(s + 1) * LANE] = _rope_slab(h, cos, sin).astype(BF16)
    qb_ref[...] = (proj(C_QB, C_KB) * (HEAD_DIM ** -0.5)).astype(BF16)

    ka = _rope_slab(proj(C_KA, C_VA), cos, sin)
    va = proj(C_VA, C_QI)
    kb = proj(C_KB, C_VB)
    vb = proj(C_VB, C_END)
    kab_ref[...] = ka.astype(BF16)
    kbb_ref[...] = kb.astype(BF16)

    misc = proj(C_MISC, C_QB)
    lane = lax.broadcasted_iota(I32, misc.shape, 1)
    kix = jnp.where(lane < D_IDX, _rope_slab(misc, cos, sin), misc)
    kix_ref[...] = kix
    misc_t = misc.T
    wit_ref[...] = misc_t[MISC_W:MISC_W + SUB, :]
    lft = _log_sigmoid(misc_t[MISC_F:MISC_F + N_HEADS_B, :] + bf_ref[...])
    lft_ref[...] = lft

    @pl.when(i % tiles_per_seq == 0)
    def _():
        carry_ref[...] = jnp.zeros_like(carry_ref)

    carry = carry_ref[...]
    cumt = _dot_exact(lft, umat_ref[...]) + jnp.concatenate([carry] * (TM // LANE), axis=1)
    carry_ref[...] = carry + _dot_exact(lft, ones_ref[...])
    terms = _split3(-cumt)
    filler = jnp.zeros((LANE - len(terms) * N_HEADS_B, TM), F32)
    cb_ref[...] = jnp.concatenate(list(terms) + [filler], axis=0).T.astype(BF16)

    @pl.when(i < n_prompt_tiles)
    def _():
        kat_ref[...] = ka.T
        vat_ref[...] = va.T
        kit_ref[...] = kix.T[:D_IDX, :]
        kbt_ref[...] = kb.T
        vbt_ref[...] = vb.T
        lf3_ref[...] = lft

    @pl.when(i >= n_prompt_tiles)
    def _():
        kas_ref[...] = ka
        vas_ref[...] = va
        kbs_ref[...] = kb
        vbs_ref[...] = vb


def _project(x, wp, cos_tab, sin_tab, bf_b, umat, ones, *, batch, seq):
    nt, d = x.shape
    n_tiles = nt // TM
    tiles_per_seq = seq // TM
    n_prompt_tiles = batch * tiles_per_seq

    def tab_map(i):
        return (jnp.where(i < n_prompt_tiles, i % tiles_per_seq, tiles_per_seq), 0)

    def umat_map(i):
        return (jnp.where(i < n_prompt_tiles, 0, 1), 0, 0)

    def seq_map(i):
        j = jnp.minimum(i, n_prompt_tiles - 1)
        return (j // tiles_per_seq, 0, j % tiles_per_seq)

    row = lambda w: pl.BlockSpec((TM, w), lambda i: (i, 0))
    col = lambda: pl.BlockSpec((SUB, TM), lambda i: (0, i))
    seqt = lambda w: pl.BlockSpec((None, w, TM), seq_map)
    srow = lambda w: pl.BlockSpec((TM, w), lambda i: (0, 0))
    sds = jax.ShapeDtypeStruct
    out_shape = (
        sds((nt, WIDTH_A), BF16), sds((nt, N_IDX_HEADS * D_IDX), BF16), sds((nt, WIDTH_B), BF16),
        sds((nt, LANE), F32), sds((SUB, nt), F32), sds((N_HEADS_B, nt), F32),
        sds((nt, KV_A), BF16), sds((nt, WIDTH_B), BF16), sds((nt, LANE), BF16),
        sds((batch, KV_A, seq), F32), sds((batch, KV_A, seq), F32), sds((batch, D_IDX, seq), F32),
        sds((batch, WIDTH_B, seq), F32), sds((batch, WIDTH_B, seq), F32), sds((batch, N_HEADS_B, seq), F32),
        sds((TM, KV_A), F32), sds((TM, KV_A), F32), sds((TM, WIDTH_B), F32), sds((TM, WIDTH_B), F32))
    return pl.pallas_call(
        functools.partial(_proj_kernel, tiles_per_seq=tiles_per_seq, n_prompt_tiles=n_prompt_tiles),
        grid=(n_tiles,),
        in_specs=[row(d), pl.BlockSpec((d, C_END), lambda i: (0, 0)),
                  pl.BlockSpec((TM, LANE), tab_map), pl.BlockSpec((TM, LANE), tab_map),
                  pl.BlockSpec((N_HEADS_B, TM), lambda i: (0, 0)),
                  pl.BlockSpec((None, TM, TM), umat_map), pl.BlockSpec((TM, LANE), lambda i: (0, 0))],
        out_specs=(row(WIDTH_A), row(N_IDX_HEADS * D_IDX), row(WIDTH_B), row(LANE), col(), col(),
                   row(KV_A), row(WIDTH_B), row(LANE),
                   seqt(KV_A), seqt(KV_A), seqt(D_IDX), seqt(WIDTH_B), seqt(WIDTH_B), seqt(N_HEADS_B),
                   srow(KV_A), srow(KV_A), srow(WIDTH_B), srow(WIDTH_B)),
        out_shape=out_shape,
        scratch_shapes=[pltpu.VMEM((N_HEADS_B, LANE), F32)],
        compiler_params=_params(("arbitrary",)),
        name="project",
    )(x, wp, cos_tab, sin_tab, bf_b, umat, ones)


def _key_block(seq):
    return min(KEY_BLOCK, seq)


def _kslice(kb, ks):
    return pl.ds(pl.multiple_of(kb * ks, ks), ks)


def _flash_heads(score_fn, value_fn, n_heads, n_blocks, m_ref, l_ref, acc_ref):
    m_ref[...] = jnp.full(m_ref.shape, NEG, F32)
    l_ref[...] = jnp.zeros_like(l_ref)
    acc_ref[...] = jnp.zeros_like(acc_ref)

    def body(kb, carry):
        scores = score_fn(kb)
        ps, alphas = [], []
        for h in range(n_heads):
            s = scores[h]
            m_old = m_ref[h]
            m_new = jnp.maximum(m_old, jnp.max(s, axis=0, keepdims=True))
            alpha = jnp.exp(m_old - m_new)
            p = jnp.exp(s - m_new)
            l_ref[h] = alpha * l_ref[h] + jnp.sum(p, axis=0, keepdims=True)
            m_ref[h] = m_new
            ps.append(p.astype(BF16))
            alphas.append(alpha)
        for h in range(n_heads):
            acc_ref[h] = alphas[h] * acc_ref[h] + _dot(value_fn(kb, h), ps[h])
        return carry
    lax.fori_loop(0, n_blocks, body, 0)
    return [acc_ref[h] / l_ref[h] for h in range(n_heads)]


def _flash_scratch(n_heads):
    return [pltpu.VMEM((n_heads, 1, TQ), F32), pltpu.VMEM((n_heads, 1, TQ), F32),
            pltpu.VMEM((n_heads, HEAD_DIM, TQ), F32)]


def _store_heads(o_ref, outs):
    for j in range(len(outs) // 2):
        pair = jnp.concatenate([outs[2 * j], outs[2 * j + 1]], axis=0)
        o_ref[:, j * LANE:(j + 1) * LANE] = pair.T.astype(BF16)


def _key_to_float(k):
    bits = jnp.where(k >= 0, k, k ^ jnp.int32(0x7FFFFFFF))
    return jnp.where(k < KEY_NEG_INF, -jnp.inf, lax.bitcast_convert_type(bits, F32))


def _chunk(c):
    return pl.ds(pl.multiple_of(c * TQ, TQ), TQ)


def _dsa_prompt_kernel(qi_ref, wit_ref, kix_ref, qa_ref, kab_ref, vat_ref, o_ref,
                       vt_ref, sc_ref, bias_ref, m_ref, l_ref, acc_ref, *, topk, ks):
    i = pl.program_id(1)
    seq = kix_ref.shape[0]
    cpb = ks // TQ
    n_blocks = i // cpb + 1
    lane = lax.broadcasted_iota(I32, (TQ, LANE), 1)
    sub = lax.broadcasted_iota(I32, (TQ, LANE), 0)
    lo = lane < HEAD_DIM
    heads_per_kv = N_HEADS_A // N_KV_A

    @pl.when(i == 0)
    def _():
        for kb in range(seq // ks):
            vt_ref[kb] = vat_ref[:, kb * ks:(kb + 1) * ks].astype(BF16)

    def query_operand(q_ref, h, half):
        slab = q_ref[:, (h // 2) * LANE:(h // 2 + 1) * LANE].astype(F32)
        if h % 2 != half:
            slab = pltpu.roll(slab, HEAD_DIM, 1)
        return jnp.where(lo == (half == 0), slab, 0.0).astype(BF16)

    qix = [query_operand(qi_ref, h, 0) for h in range(N_IDX_HEADS)]
    qpos = i * TQ + lane

    def score_block(kb, carry):
        for cc in range(cpb):
            c = kb * cpb + cc
            r = _chunk(c)
            kk = kix_ref[r, :].astype(BF16)
            acc = jnp.zeros((TQ, LANE), F32)
            for h in range(N_IDX_HEADS):
                acc = acc + wit_ref[h:h + 1, :] * jnp.maximum(_dot_nt(kk, qix[h]), 0.0)
            sc_ref[r, :] = jnp.where(c * TQ + sub > qpos, -jnp.inf, acc)
        return carry
    lax.fori_loop(0, n_blocks, score_block, 0)

    def count(pred):
        def body(kb, acc):
            for cc in range(cpb):
                acc = acc + jnp.where(pred(sc_ref[_chunk(kb * cpb + cc), :]), 1.0, 0.0)
            return acc
        return jnp.sum(lax.fori_loop(0, n_blocks, body, jnp.zeros((TQ, LANE), F32)), axis=0, keepdims=True)

    def bit_step(t, key):
        cand = key ^ (jnp.int32(1) << (31 - t))
        cand_f = _key_to_float(cand)
        return jnp.where(count(lambda s: s >= cand_f) >= topk, cand, key)
    thr = _key_to_float(lax.fori_loop(0, 32, bit_step, jnp.full((1, LANE), INT_MIN, I32)))
    need = topk - count(lambda s: s > thr)

    ltri = jnp.where(lane < sub, 1.0, 0.0).astype(BF16)

    def mask_block(kb, n_eq):
        for cc in range(cpb):
            c = kb * cpb + cc
            r = _chunk(c)
            sc = sc_ref[r, :]
            eq = sc == thr
            eqf = jnp.where(eq, 1.0, 0.0)
            before = _dot(ltri, eqf.astype(BF16)) + n_eq
            sel = (sc > thr) | (eq & (before < need))
            sel = sel & (c * TQ + sub <= qpos)
            bias_ref[r, :] = jnp.where(sel, 0.0, NEG)
            n_eq = n_eq + jnp.sum(eqf, axis=0, keepdims=True)
        return n_eq
    lax.fori_loop(0, n_blocks, mask_block, jnp.zeros((1, LANE), F32))

    qx = [query_operand(qa_ref, h, h // heads_per_kv) for h in range(N_HEADS_A)]

    def scores(kb):
        r = _kslice(kb, ks)
        kk = kab_ref[r, :]
        bias = bias_ref[r, :]
        return [_dot_nt(kk, qx[h]) + bias for h in range(N_HEADS_A)]

    def values(kb, h):
        g = h // heads_per_kv
        return vt_ref[kb, g * HEAD_DIM:(g + 1) * HEAD_DIM, :]

    _store_heads(o_ref, _flash_heads(scores, values, N_HEADS_A, n_blocks, m_ref, l_ref, acc_ref))


def _dsa_prompt(qi, wit, kix, qa, kab, vat, *, batch, seq):
    topk = min(TOPK_MAX, seq // 4)
    nq = seq // TQ
    ks = _key_block(seq)
    qblk = lambda w: pl.BlockSpec((TQ, w), lambda b, i: (b * nq + i, 0))
    seqblk = lambda w: pl.BlockSpec((seq, w), lambda b, i: (b, 0))
    return pl.pallas_call(
        functools.partial(_dsa_prompt_kernel, topk=float(topk), ks=ks),
        grid=(batch, nq),
        in_specs=[qblk(N_IDX_HEADS * D_IDX), pl.BlockSpec((SUB, TQ), lambda b, i: (0, b * nq + i)),
                  seqblk(LANE), qblk(WIDTH_A), seqblk(KV_A),
                  pl.BlockSpec((None, KV_A, seq), lambda b, i: (b, 0, 0))],
        out_specs=qblk(WIDTH_A),
        out_shape=jax.ShapeDtypeStruct((batch * seq, WIDTH_A), BF16),
        scratch_shapes=[pltpu.VMEM((seq // ks, KV_A, ks), BF16), pltpu.VMEM((seq, LANE), F32),
                        pltpu.VMEM((seq, LANE), F32)] + _flash_scratch(N_HEADS_A),
        compiler_params=_params(("arbitrary", "arbitrary")),
        name="dsa_prompt",
    )(qi, wit, kix, qa, kab, vat)


def _fox_prompt_kernel(qb_ref, kbb_ref, cb_ref, vbt_ref, o_ref, vt_ref, m_ref, l_ref, acc_ref, *, ks):
    i = pl.program_id(1)
    seq = kbb_ref.shape[0]
    n_blocks = i // (ks // TQ) + 1
    lane = lax.broadcasted_iota(I32, (TQ, LANE), 1)

    @pl.when(i == 0)
    def _():
        for kb in range(seq // ks):
            vt_ref[kb] = vbt_ref[:, kb * ks:(kb + 1) * ks].astype(BF16)

    qx = []
    for h in range(N_HEADS_B):
        slab = qb_ref[:, (h // 2) * LANE:(h // 2 + 1) * LANE]
        qh = jnp.where((lane >= HEAD_DIM) == (h % 2 == 1), slab, jnp.zeros_like(slab))
        pick = jnp.where((lane % N_HEADS_B == h) & (lane < 3 * N_HEADS_B), 1.0, 0.0).astype(BF16)
        qx.append(jnp.concatenate([qh, pick], axis=1))

    kpos = lax.broadcasted_iota(I32, (ks, TQ), 0)
    qpos = i * TQ + lax.broadcasted_iota(I32, (ks, TQ), 1)

    def scores(kb):
        r = _kslice(kb, ks)
        cols = cb_ref[r, :]
        hidden = kb * ks + kpos > qpos
        out = []
        for j in range(N_HEADS_B // 2):
            kx = jnp.concatenate([kbb_ref[r, j * LANE:(j + 1) * LANE], cols], axis=1)
            for par in range(2):
                out.append(jnp.where(hidden, NEG, _dot_nt(kx, qx[2 * j + par])))
        return out

    def values(kb, h):
        return vt_ref[kb, h * HEAD_DIM:(h + 1) * HEAD_DIM, :]

    _store_heads(o_ref, _flash_heads(scores, values, N_HEADS_B, n_blocks, m_ref, l_ref, acc_ref))


def _fox_prompt(qb, kbb, cb, vbt, *, batch, seq):
    nq = seq // TQ
    ks = _key_block(seq)
    qblk = pl.BlockSpec((TQ, WIDTH_B), lambda b, i: (b * nq + i, 0))
    seqblk = lambda w: pl.BlockSpec((seq, w), lambda b, i: (b, 0))
    return pl.pallas_call(
        functools.partial(_fox_prompt_kernel, ks=ks),
        grid=(batch, nq),
        in_specs=[qblk, seqblk(WIDTH_B), seqblk(LANE), pl.BlockSpec((None, WIDTH_B, seq), lambda b, i: (b, 0, 0))],
        out_specs=qblk,
        out_shape=jax.ShapeDtypeStruct((batch * seq, WIDTH_B), BF16),
        scratch_shapes=[pltpu.VMEM((seq // ks, WIDTH_B, ks), BF16)] + _flash_scratch(N_HEADS_B),
        compiler_params=_params(("arbitrary", "arbitrary")),
        name="fox_prompt",
    )(qb, kbb, cb, vbt)


def _rows_to_heads(x, n_heads):
    return jnp.concatenate([x] * n_heads, axis=0)


def _softmax_step(qm, ks, vs, biases, m_ref, l_ref, acc_ref, *, paged):
    qk, pv = (_dot, _dot_nt) if paged else (_dot_nt, _dot)
    logits = [qk(qm, k) + b for k, b in zip(ks, biases)]
    smax = functools.reduce(jnp.maximum, logits)
    m_old = m_ref[...]
    m_new = jnp.maximum(m_old, jnp.max(smax, axis=1, keepdims=True))
    alpha = jnp.exp(m_old - m_new)
    ps = [jnp.exp(s - m_new) for s in logits]
    l_ref[...] = alpha * l_ref[...] + jnp.sum(functools.reduce(jnp.add, ps), axis=1, keepdims=True)
    upd = functools.reduce(jnp.add, [pv(p.astype(BF16), v) for p, v in zip(ps, vs)])
    acc_ref[...] = alpha * acc_ref[...] + upd
    m_ref[...] = m_new


def _pad_rows(x):
    return jnp.concatenate([x, jnp.zeros((TQ - x.shape[0], x.shape[1]), x.dtype)], axis=0)


def _new_key_bias(n_rows, n_new):
    lane = lax.broadcasted_iota(I32, (n_rows, LANE), 1)
    t = lax.broadcasted_iota(I32, (n_rows, LANE), 0) % n_new
    return jnp.where((lane <= t) & (lane < n_new), 0.0, NEG)


def _dsa_select_kernel(pt_ref, qi_ref, kix_ref, *rest, topk, n_pages):
    g_pages = PAGES_PER_STEP
    page_refs = rest[:g_pages]
    o_ref = rest[g_pages]
    qim_ref, w_ref, sc_ref = rest[g_pages + 1:]
    j = pl.program_id(1)
    nj = pl.num_programs(1)
    n_q = qi_ref.shape[0]
    nch = sc_ref.shape[0]

    @pl.when(j == 0)
    def _():
        pieces, ws = [], []
        kix = kix_ref[...]
        for h in range(N_IDX_HEADS):
            slab = qi_ref[:, (h // 2) * LANE:(h // 2 + 1) * LANE].astype(F32)
            if h % 2:
                slab = pltpu.roll(slab, HEAD_DIM, 1)
            pieces.append(slab[:, :D_IDX])
            ws.append(jnp.broadcast_to(kix[:, MISC_W + h:MISC_W + h + 1], (n_q, LANE)))
        qim_ref[...] = jnp.concatenate(pieces, axis=0).astype(BF16)
        w_ref[...] = jnp.concatenate(ws, axis=0)
        sc_ref[n_pages + 1:] = jnp.full((nch - n_pages - 1, n_q, LANE), -jnp.inf, F32)

    def scores(qk):
        r = jnp.maximum(qk, 0.0) * w_ref[...]
        return functools.reduce(jnp.add, [r[h * n_q:(h + 1) * n_q] for h in range(N_IDX_HEADS)])

    for g in range(g_pages):
        sc_ref[j * g_pages + g] = scores(_dot(qim_ref[...], page_refs[g][...].astype(BF16)))

    @pl.when(j == nj - 1)
    def _():
        new = scores(_dot_nt(qim_ref[...], _pad_rows(kix_ref[:, :D_IDX]).astype(BF16)))
        vis = _new_key_bias(n_q, n_q) == 0.0
        sc_ref[n_pages] = jnp.where(vis, new, -jnp.inf)

        def count(pred):
            hits = jnp.sum(jnp.where(pred(sc_ref[...]), 1.0, 0.0), axis=0)
            return jnp.sum(hits, axis=1, keepdims=True)

        def bit_step(t, key):
            cand = key ^ (jnp.int32(1) << (31 - t))
            cand_f = _key_to_float(cand)
            return jnp.where(count(lambda s: s >= cand_f[None]) >= topk, cand, key)
        thr = _key_to_float(lax.fori_loop(0, 32, bit_step, jnp.full((n_q, 1), INT_MIN, I32)))
        need = topk - count(lambda s: s > thr[None])

        n_rows = nch * n_q
        keys = sc_ref[...]
        eq = keys == thr[None]
        eq2 = jnp.where(eq, 1.0, 0.0).reshape(n_rows, LANE).astype(BF16)
        lane = lax.broadcasted_iota(I32, (LANE, LANE), 1)
        sub = lax.broadcasted_iota(I32, (LANE, LANE), 0)
        within = _dot(eq2, jnp.where(sub < lane, 1.0, 0.0).astype(BF16))
        total = _dot(eq2, jnp.ones((LANE, LANE), BF16))
        rr = lax.broadcasted_iota(I32, (n_rows, n_rows), 0)
        cc = lax.broadcasted_iota(I32, (n_rows, n_rows), 1)
        earlier = jnp.where((cc < rr) & ((rr - cc) % n_q == 0), 1.0, 0.0).astype(BF16)
        before = (within + _dot(earlier, total.astype(BF16))).reshape(nch, n_q, LANE)
        chunk = lax.broadcasted_iota(I32, (nch, n_q, LANE), 0)
        sel = (keys > thr[None]) | (eq & (before < need[None]))
        sel = sel & ((chunk < n_pages) | ((chunk == n_pages) & vis[None]))
        o_ref[...] = jnp.where(sel, 0.0, NEG)[:n_pages + 1]


def _dsa_select(page_table, qi, kix, pool_kidx, layer, *, n_prompt, dec_batch, n_new):
    n_pages = page_table.shape[1]
    g = PAGES_PER_STEP
    nj = n_pages // g
    topk = min(TOPK_MAX, (n_pages * PAGE_SIZE + n_new) // 4)
    base = n_prompt // n_new
    row = lambda w: pl.BlockSpec((n_new, w), lambda b, j, pt: (base + b, 0))

    def page_spec(gi):
        return pl.BlockSpec((None, None, D_IDX, PAGE_SIZE), lambda b, j, pt: (layer, pt[b, j * g + gi], 0, 0))

    n_chunks = -(-(n_pages + 1) // (LANE // n_new)) * (LANE // n_new)
    grid_spec = pltpu.PrefetchScalarGridSpec(
        num_scalar_prefetch=1, grid=(dec_batch, nj),
        in_specs=[row(N_IDX_HEADS * D_IDX), row(LANE)] + [page_spec(gi) for gi in range(g)],
        out_specs=pl.BlockSpec((None, n_pages + 1, n_new, LANE), lambda b, j, pt: (b, 0, 0, 0)),
        scratch_shapes=[pltpu.VMEM((N_IDX_HEADS * n_new, D_IDX), BF16), pltpu.VMEM((N_IDX_HEADS * n_new, LANE), F32),
                        pltpu.VMEM((n_chunks, n_new, LANE), F32)])
    return pl.pallas_call(
        functools.partial(_dsa_select_kernel, topk=float(topk), n_pages=n_pages),
        grid_spec=grid_spec,
        out_shape=jax.ShapeDtypeStruct((dec_batch, n_pages + 1, n_new, LANE), F32),
        compiler_params=_params(("arbitrary", "arbitrary")),
        name="dsa_select",
    )(page_table, qi, kix, *([pool_kidx] * g))


def _dsa_attend_kernel(pt_ref, qa_ref, knew_ref, vnew_ref, sel_ref, selnew_ref, *rest):
    g_pages = PAGES_PER_STEP
    k_refs = rest[:g_pages]
    v_refs = rest[g_pages:2 * g_pages]
    o_ref = rest[2 * g_pages]
    qm_ref, m_ref, l_ref, acc_ref = rest[2 * g_pages + 1:]
    j = pl.program_id(1)
    nj = pl.num_programs(1)
    n_q = qa_ref.shape[0]
    heads_per_kv = N_HEADS_A // N_KV_A
    lane = lax.broadcasted_iota(I32, (n_q, LANE), 1)
    lo = lane < HEAD_DIM

    @pl.when(j == 0)
    def _():
        pieces = []
        for h in range(N_HEADS_A):
            slab = qa_ref[:, (h // 2) * LANE:(h // 2 + 1) * LANE].astype(F32)
            kv = h // heads_per_kv
            if h % 2 != kv:
                slab = pltpu.roll(slab, HEAD_DIM, 1)
            pieces.append(jnp.where(lo == (kv == 0), slab, 0.0))
        qm_ref[...] = jnp.concatenate(pieces, axis=0).astype(BF16)
        m_ref[...] = jnp.full(m_ref.shape, NEG, F32)
        l_ref[...] = jnp.zeros_like(l_ref)
        acc_ref[...] = jnp.zeros_like(acc_ref)

    qm = qm_ref[...]
    _softmax_step(qm, [k[...].astype(BF16) for k in k_refs], [v[...].astype(BF16) for v in v_refs],
                  [_rows_to_heads(sel_ref[g], N_HEADS_A) for g in range(g_pages)], m_ref, l_ref, acc_ref, paged=True)

    @pl.when(j == nj - 1)
    def _():
        _softmax_step(qm, [_pad_rows(knew_ref[...]).astype(BF16)], [_pad_rows(vnew_ref[...]).astype(BF16)],
                      [_rows_to_heads(selnew_ref[...], N_HEADS_A)], m_ref, l_ref, acc_ref, paged=False)
        out = acc_ref[...] / l_ref[...]
        for s in range(N_HEADS_A // 2):
            a0 = out[(2 * s) * n_q:(2 * s + 1) * n_q]
            a1 = out[(2 * s + 1) * n_q:(2 * s + 2) * n_q]
            if (2 * s) // heads_per_kv == 0:
                slab = jnp.where(lo, a0, pltpu.roll(a1, HEAD_DIM, 1))
            else:
                slab = jnp.where(lo, pltpu.roll(a0, HEAD_DIM, 1), a1)
            o_ref[:, s * LANE:(s + 1) * LANE] = slab.astype(BF16)


def _dsa_attend(page_table, qa, ka, va, sel, pool_k, pool_v, layer, *, n_prompt, dec_batch, n_new):
    n_pages = page_table.shape[1]
    g = PAGES_PER_STEP
    nj = n_pages // g
    base = n_prompt // n_new
    row = lambda w: pl.BlockSpec((n_new, w), lambda b, j, pt: (base + b, 0))
    new = lambda w: pl.BlockSpec((n_new, w), lambda b, j, pt: (b, 0))

    def page_spec(gi):
        return pl.BlockSpec((None, None, KV_A, PAGE_SIZE), lambda b, j, pt: (layer, pt[b, j * g + gi], 0, 0))

    grid_spec = pltpu.PrefetchScalarGridSpec(
        num_scalar_prefetch=1, grid=(dec_batch, nj),
        in_specs=[row(WIDTH_A), new(KV_A), new(KV_A),
                  pl.BlockSpec((None, g, n_new, LANE), lambda b, j, pt: (b, j, 0, 0)),
                  pl.BlockSpec((None, None, n_new, LANE), lambda b, j, pt: (b, n_pages, 0, 0))]
        + [page_spec(gi) for gi in range(g)] * 2,
        out_specs=pl.BlockSpec((n_new, WIDTH_A), lambda b, j, pt: (b, 0)),
        scratch_shapes=[pltpu.VMEM((N_HEADS_A * n_new, KV_A), BF16), pltpu.VMEM((N_HEADS_A * n_new, 1), F32),
                        pltpu.VMEM((N_HEADS_A * n_new, 1), F32), pltpu.VMEM((N_HEADS_A * n_new, KV_A), F32)])
    return pl.pallas_call(
        _dsa_attend_kernel, grid_spec=grid_spec,
        out_shape=jax.ShapeDtypeStruct((dec_batch * n_new, WIDTH_A), BF16),
        compiler_params=_params(("arbitrary", "arbitrary")),
        name="dsa_attend",
    )(page_table, qa, ka, va, sel, sel, *([pool_k] * g), *([pool_v] * g))


def _fox_sample_kernel(pt_ref, qb_ref, knew_ref, vnew_ref, lfnew_ref, *rest):
    g_pages = PAGES_PER_STEP
    k_refs = rest[:g_pages]
    v_refs = rest[g_pages:2 * g_pages]
    lf_refs = rest[2 * g_pages:3 * g_pages]
    o_ref = rest[3 * g_pages]
    qm_ref, m_ref, l_ref, acc_ref, carry_ref = rest[3 * g_pages + 1:]
    j = pl.program_id(1)
    nj = pl.num_programs(1)
    n_q = qb_ref.shape[0]
    n_rows = N_HEADS_B * n_q

    @pl.when(j == 0)
    def _():
        q = jnp.concatenate([qb_ref[...]] * N_HEADS_B, axis=0)
        head_of_lane = lax.broadcasted_iota(I32, q.shape, 1) // HEAD_DIM
        head_of_row = lax.broadcasted_iota(I32, q.shape, 0) // n_q
        qm_ref[...] = jnp.where(head_of_lane == head_of_row, q, jnp.zeros_like(q))
        m_ref[...] = jnp.full(m_ref.shape, NEG, F32)
        l_ref[...] = jnp.zeros_like(l_ref)
        acc_ref[...] = jnp.zeros_like(acc_ref)
        carry_ref[...] = jnp.zeros_like(carry_ref)

    lane = lax.broadcasted_iota(I32, (LANE, LANE), 1)
    sub = lax.broadcasted_iota(I32, (LANE, LANE), 0)
    u_incl = jnp.where(sub <= lane, 1.0, 0.0)
    ones = jnp.ones((LANE, LANE), F32)

    def bias_chunks(lfs):
        out = []
        carry = carry_ref[...]
        for lf in lfs:
            rows = jnp.concatenate([jnp.broadcast_to(lf[h:h + 1, :], (n_q, LANE)) for h in range(N_HEADS_B)], axis=0)
            out.append(-(carry + _dot_exact(rows, u_incl)))
            carry = carry + _dot_exact(rows, ones)
        carry_ref[...] = carry
        return out

    qm = qm_ref[...]
    _softmax_step(qm, [k[...].astype(BF16) for k in k_refs], [v[...].astype(BF16) for v in v_refs],
                  bias_chunks([r[...] for r in lf_refs]), m_ref, l_ref, acc_ref, paged=True)

    @pl.when(j == nj - 1)
    def _():
        bias = bias_chunks([lfnew_ref[...]])[0] + _new_key_bias(n_rows, n_q)
        _softmax_step(qm, [_pad_rows(knew_ref[...]).astype(BF16)], [_pad_rows(vnew_ref[...]).astype(BF16)],
                      [bias], m_ref, l_ref, acc_ref, paged=False)
        out = acc_ref[...] / l_ref[...]
        head_of_lane = lax.broadcasted_iota(I32, (n_q, WIDTH_B), 1) // HEAD_DIM
        res = jnp.zeros((n_q, WIDTH_B), F32)
        for h in range(N_HEADS_B):
            res = res + jnp.where(head_of_lane == h, out[h * n_q:(h + 1) * n_q], 0.0)
        o_ref[...] = res.astype(BF16)


def _fox_sample(page_table, qb, kb, vb, lf_new, pool_k, pool_v, pool_lft, layer, *, n_prompt, dec_batch, n_new):
    n_pages = page_table.shape[1]
    g = PAGES_PER_STEP
    nj = n_pages // g
    base = n_prompt // n_new
    row = lambda w: pl.BlockSpec((n_new, w), lambda b, j, pt: (base + b, 0))
    new = lambda w: pl.BlockSpec((n_new, w), lambda b, j, pt: (b, 0))

    def page_spec(gi, rows, w):
        return pl.BlockSpec((None, None, rows, w), lambda b, j, pt: (layer, pt[b, j * g + gi], 0, 0))

    n_rows = N_HEADS_B * n_new
    grid_spec = pltpu.PrefetchScalarGridSpec(
        num_scalar_prefetch=1, grid=(dec_batch, nj),
        in_specs=[row(WIDTH_B), new(WIDTH_B), new(WIDTH_B),
                  pl.BlockSpec((None, N_HEADS_B, LANE), lambda b, j, pt: (b, 0, 0))]
        + [page_spec(gi, WIDTH_B, PAGE_SIZE) for gi in range(g)] * 2
        + [page_spec(gi, N_HEADS_B, PAGE_SIZE) for gi in range(g)],
        out_specs=pl.BlockSpec((n_new, WIDTH_B), lambda b, j, pt: (b, 0)),
        scratch_shapes=[pltpu.VMEM((n_rows, WIDTH_B), BF16), pltpu.VMEM((n_rows, 1), F32),
                        pltpu.VMEM((n_rows, 1), F32), pltpu.VMEM((n_rows, WIDTH_B), F32),
                        pltpu.VMEM((n_rows, LANE), F32)])
    return pl.pallas_call(
        _fox_sample_kernel, grid_spec=grid_spec,
        out_shape=jax.ShapeDtypeStruct((dec_batch * n_new, WIDTH_B), BF16),
        compiler_params=_params(("arbitrary", "arbitrary")),
        name="fox_sample",
    )(page_table, qb, kb, vb, lf_new, *([pool_k] * g), *([pool_v] * g), *([pool_lft] * g))


def _merge_kernel(x_ref, oa_ref, ob_ref, wg_ref, wa_ref, wb_ref, wo_ref, g_ref, b_ref, *rest, alpha, d_model):
    x = x_ref[...]
    xb = x.astype(BF16)
    ua = _dot(oa_ref[...], wa_ref[...])
    ub = _dot(ob_ref[...], wb_ref[...])
    merged = jax.nn.sigmoid(_dot(xb, wg_ref[:, :d_model])) * ua + jax.nn.sigmoid(_dot(xb, wg_ref[:, d_model:])) * ub
    mix = _dot(merged.astype(BF16), wo_ref[...])
    y = _layer_norm(alpha * x + mix, g_ref[...], b_ref[...])
    if len(rest) == 1:
        rest[0][...] = y
    else:
        wr_ref, y_ref, yb_ref, lg_ref = rest
        y_ref[...] = y
        yb = y.astype(BF16)
        yb_ref[...] = yb
        lg_ref[...] = _dot(yb, wr_ref[...])


def _merge(x, oa, ob, wg, wa, wb, wo, g, b, wr, *, alpha):
    nt, d = x.shape
    row = lambda w: pl.BlockSpec((TM, w), lambda i: (i, 0))
    full = lambda a: pl.BlockSpec(a.shape, lambda i: (0,) * a.ndim)
    ins = [x, oa, ob, wg, wa, wb, wo, g, b]
    in_specs = [row(d), row(WIDTH_A), row(WIDTH_B)] + [full(a) for a in ins[3:]]
    if wr is None:
        out_shape = jax.ShapeDtypeStruct((nt, d), F32)
        out_specs = row(d)
    else:
        ins.append(wr)
        in_specs.append(full(wr))
        out_shape = (jax.ShapeDtypeStruct((nt, d), F32), jax.ShapeDtypeStruct((nt, d), BF16),
                     jax.ShapeDtypeStruct((nt, LANE), F32))
        out_specs = (row(d), row(d), row(LANE))
    return pl.pallas_call(
        functools.partial(_merge_kernel, alpha=alpha, d_model=d),
        grid=(nt // TM,), in_specs=in_specs, out_specs=out_specs, out_shape=out_shape,
        compiler_params=_params(("arbitrary",)),
        name="merge",
    )(*ins)


def _swiglu(xb, wg_ref, wu_ref, wd_ref):
    d_ff = wg_ref.shape[-1]
    acc = None
    for c in range(0, d_ff, FF_CHUNK):
        h = jax.nn.silu(_dot(xb, wg_ref[:, c:c + FF_CHUNK])) * _dot(xb, wu_ref[:, c:c + FF_CHUNK])
        part = _dot(h.astype(BF16), wd_ref[c:c + FF_CHUNK, :])
        acc = part if acc is None else acc + part
    return acc


def _ffn_kernel(x_ref, wg_ref, wu_ref, wd_ref, g_ref, b_ref, y_ref, *, alpha):
    x = x_ref[...]
    f = _swiglu(x.astype(BF16), wg_ref, wu_ref, wd_ref)
    y_ref[...] = _layer_norm(alpha * x + f, g_ref[...], b_ref[...])


def _ffn(x, wg, wu, wd, g, b, *, alpha):
    nt, d = x.shape
    row = pl.BlockSpec((TM, d), lambda i: (i, 0))
    full = lambda a: pl.BlockSpec(a.shape, lambda i: (0,) * a.ndim, pipeline_mode=pl.Buffered(1))
    return pl.pallas_call(
        functools.partial(_ffn_kernel, alpha=alpha),
        grid=(nt // TM,), in_specs=[row, full(wg), full(wu), full(wd), full(g), full(b)], out_specs=row,
        out_shape=jax.ShapeDtypeStruct((nt, d), F32),
        compiler_params=_params(("arbitrary",)),
        name="ffn",
    )(x, wg, wu, wd, g, b)


def _moe_kernel(te_ref, nu_ref, xs_ref, gate_ref, wg_ref, wu_ref, wd_ref, o_ref):
    @pl.when(pl.program_id(0) < nu_ref[0])
    def _():
        o_ref[...] = gate_ref[...] * _swiglu(xs_ref[...], wg_ref, wu_ref, wd_ref)

    @pl.when(pl.program_id(0) >= nu_ref[0])
    def _():
        o_ref[...] = jnp.zeros_like(o_ref)


def _moe_experts(tile_expert, n_used, xs, gates, wg, wu, wd):
    rows, d = xs.shape
    d_ff = wg.shape[-1]
    grid_spec = pltpu.PrefetchScalarGridSpec(
        num_scalar_prefetch=2, grid=(rows // TM,),
        in_specs=[pl.BlockSpec((TM, d), lambda i, te, nu: (i, 0)), pl.BlockSpec((TM, 1), lambda i, te, nu: (i, 0)),
                  pl.BlockSpec((None, d, d_ff), lambda i, te, nu: (te[i], 0, 0)),
                  pl.BlockSpec((None, d, d_ff), lambda i, te, nu: (te[i], 0, 0)),
                  pl.BlockSpec((None, d_ff, d), lambda i, te, nu: (te[i], 0, 0))],
        out_specs=pl.BlockSpec((TM, d), lambda i, te, nu: (i, 0)))
    return pl.pallas_call(
        _moe_kernel, grid_spec=grid_spec, out_shape=jax.ShapeDtypeStruct((rows, d), F32),
        compiler_params=_params(("arbitrary",)),
        name="moe_experts",
    )(tile_expert, n_used, xs, gates, wg, wu, wd)


def _combine_kernel(x_ref, o1_ref, o2_ref, g_ref, b_ref, y_ref, *, alpha):
    y_ref[...] = _layer_norm(alpha * x_ref[...] + (o1_ref[...] + o2_ref[...]), g_ref[...], b_ref[...])


def _combine(x, o1, o2, g, b, *, alpha):
    nt, d = x.shape
    row = pl.BlockSpec((TM, d), lambda i: (i, 0))
    full = lambda a: pl.BlockSpec(a.shape, lambda i: (0,) * a.ndim)
    return pl.pallas_call(
        functools.partial(_combine_kernel, alpha=alpha),
        grid=(nt // TM,), in_specs=[row, row, row, full(g), full(b)], out_specs=row,
        out_shape=jax.ShapeDtypeStruct((nt, d), F32),
        compiler_params=_params(("arbitrary",)),
        name="moe_combine",
    )(x, o1, o2, g, b)


def _route(logits, n_rows):
    nt = logits.shape[0]
    top_v, top_i = lax.top_k(logits[:, :N_EXPERTS], TOP_K_EXPERTS)
    top_w = jax.nn.softmax(top_v, axis=-1)
    e_flat = top_i.reshape(-1)
    onehot = (e_flat[:, None] == jnp.arange(N_EXPERTS, dtype=I32)[None, :]).astype(I32)
    rank = jnp.sum((jnp.cumsum(onehot, axis=0) - onehot) * onehot, axis=1)
    counts = jnp.sum(onehot, axis=0)
    padded = ((counts + TM - 1) // TM) * TM
    ends = jnp.cumsum(padded)
    dest = (ends - padded)[e_flat] + rank
    token = jnp.arange(nt * TOP_K_EXPERTS, dtype=I32) // TOP_K_EXPERTS
    row_token = jnp.zeros((n_rows,), I32).at[dest].set(token)
    row_gate = jnp.zeros((n_rows,), F32).at[dest].set(top_w.reshape(-1))
    tile_start = jnp.arange(n_rows // TM, dtype=I32) * TM
    tile_expert = jnp.minimum(jnp.sum((ends[None, :] <= tile_start[:, None]).astype(I32), axis=1), N_EXPERTS - 1)
    n_used = (ends[-1] // TM).astype(I32).reshape(1)
    return dest.reshape(nt, TOP_K_EXPERTS), row_token, row_gate.reshape(n_rows, 1), tile_expert, n_used


def kernel(x_prompt, x_sample, cache_k_a, cache_v_a, cache_kidx_a, cache_k_b, cache_v_b, cache_logf_b,
           page_table, w_in, b_f, w_br_a, w_br_b, w_out, ln1_g, ln1_b, ln2_g, ln2_b,
           w_gate_d, w_up_d, w_down_d, w_router, w_gate_e, w_up_e, w_down_e):
    batch, seq, d = x_prompt.shape
    dec_batch, n_new, _ = x_sample.shape
    depth = w_in.shape[0]
    n_pool = cache_k_a.shape[1]
    n_pages = page_table.shape[1]
    past = n_pages * PAGE_SIZE
    n_prompt = batch * seq
    n_sample = dec_batch * n_new
    nt = n_prompt + n_sample
    assert seq % TM == 0 and n_sample == TM and n_pages % PAGES_PER_STEP == 0 and n_new == SUB
    alpha = (2.0 * depth) ** 0.25

    splits = [s if s is not None else d for s in IN_SPLITS]
    offs = np.cumsum([0] + splits)
    col = lambda k: w_in[:, :, offs[k]:offs[k + 1]]
    pad = jnp.zeros((depth, d, LANE - D_IDX - N_HEADS_B - N_IDX_HEADS), w_in.dtype)
    wp = jnp.concatenate([col(0), col(1), col(2), col(3), col(4), col(9), col(5), pad, col(6), col(7), col(8)],
                         axis=-1).astype(BF16)
    wgates = w_in[:, :, offs[10]:].astype(BF16)
    wa, wb, wo = w_br_a.astype(BF16), w_br_b.astype(BF16), w_out.astype(BF16)
    wgd, wud, wdd = w_gate_d.astype(BF16), w_up_d.astype(BF16), w_down_d.astype(BF16)
    wge, wue, wde = w_gate_e.astype(BF16), w_up_e.astype(BF16), w_down_e.astype(BF16)
    wr = jnp.pad(w_router, ((0, 0), (0, 0), (0, LANE - N_EXPERTS))).astype(BF16)
    row2 = lambda a: a.reshape(depth, 1, d)
    g1, b1, g2, b2 = row2(ln1_g), row2(ln1_b), row2(ln2_g), row2(ln2_b)
    bf_b = jnp.broadcast_to(b_f[:, :, None], (depth, N_HEADS_B, TM))

    inv = ROPE_THETA ** (-jnp.arange(0, HEAD_DIM, 2, dtype=F32) / HEAD_DIM)
    pos = jnp.concatenate([jnp.arange(seq, dtype=I32), past + jnp.arange(TM, dtype=I32) % n_new]).astype(F32)
    ang = pos[:, None] * inv[None, :]
    cos_tab = jnp.tile(jnp.cos(ang), (1, LANE // (HEAD_DIM // 2)))
    sin_tab = jnp.tile(jnp.concatenate([-jnp.sin(ang), jnp.sin(ang)], axis=1), (1, LANE // HEAD_DIM))
    r = jnp.arange(TM)
    upper = r[:, None] <= r[None, :]
    umat = jnp.stack([upper, upper & (r[:, None] // n_new == r[None, :] // n_new)]).astype(F32)
    ones = jnp.ones((TM, LANE), F32)

    def pages_t(c):
        c = jnp.moveaxis(c, 2, -1)
        return c.reshape(depth, n_pool, -1, PAGE_SIZE)
    pool_ki, pool_ka, pool_va = pages_t(cache_kidx_a), pages_t(cache_k_a), pages_t(cache_v_a)
    pool_kb, pool_vb, pool_lft = pages_t(cache_k_b), pages_t(cache_v_b), pages_t(cache_logf_b)

    n_moe_rows = ((nt * TOP_K_EXPERTS + N_EXPERTS * (TM - 1)) // TM + 1) * TM

    x = jnp.concatenate([x_prompt.reshape(n_prompt, d), x_sample.reshape(n_sample, d)], axis=0)
    prompt_rows, sample_rows = [], []
    for l in range(depth):
        (qa, qi, qb, kix, wit, lft, kab, kbb, cb, kat, vat, kit, kbt, vbt, lf3, kas, vas, kbs, vbs) = _project(
            x, wp[l], cos_tab, sin_tab, bf_b[l], umat, ones, batch=batch, seq=seq)

        oa_p = _dsa_prompt(qi, wit, kix, qa, kab, vat, batch=batch, seq=seq)
        ob_p = _fox_prompt(qb, kbb, cb, vbt, batch=batch, seq=seq)

        skw = dict(n_prompt=n_prompt, dec_batch=dec_batch, n_new=n_new)
        sel = _dsa_select(page_table, qi, kix, pool_ki, l, **skw)
        oa_s = _dsa_attend(page_table, qa, kas, vas, sel, pool_ka, pool_va, l, **skw)
        lf_s = lft[:, n_prompt:]
        lf_new = jnp.pad(lf_s.reshape(N_HEADS_B, dec_batch, n_new).transpose(1, 0, 2), ((0, 0), (0, 0), (0, LANE - n_new)))
        ob_s = _fox_sample(page_table, qb, kbs, vbs, lf_new, pool_kb, pool_vb, pool_lft, l, **skw)

        oa = jnp.concatenate([oa_p, oa_s], axis=0)
        ob = jnp.concatenate([ob_p, ob_s], axis=0)
        if l % 2 == 0:
            x1 = _merge(x, oa, ob, wgates[l], wa[l], wb[l], wo[l], g1[l], b1[l], None, alpha=alpha)
            x = _ffn(x1, wgd[l // 2], wud[l // 2], wdd[l // 2], g2[l], b2[l], alpha=alpha)
        else:
            x1, x1b, logits = _merge(x, oa, ob, wgates[l], wa[l], wb[l], wo[l], g1[l], b1[l], wr[l // 2], alpha=alpha)
            dest, row_token, row_gate, tile_expert, n_used = _route(logits, n_moe_rows)
            ys = _moe_experts(tile_expert, n_used, x1b[row_token], row_gate, wge[l // 2], wue[l // 2], wde[l // 2])
            x = _combine(x1, ys[dest[:, 0]], ys[dest[:, 1]], g2[l], b2[l], alpha=alpha)
        prompt_rows.append((kat, vat, kit, kbt, vbt, lf3))
        sample_rows.append((kas, vas, kix[n_prompt:, :D_IDX], kbs, vbs, lf_s.T))

    tails = ((N_KV_A, HEAD_DIM), (N_KV_A, HEAD_DIM), (D_IDX,), (N_HEADS_B, HEAD_DIM), (N_HEADS_B, HEAD_DIM), (N_HEADS_B,))
    prompt_out, sample_out = [], []
    for k, tail in enumerate(tails):
        p = jnp.stack([r[k] for r in prompt_rows]).reshape((depth, batch) + tail + (seq,))
        prompt_out.append(jnp.moveaxis(p, -1, 2))
        sample_out.append(jnp.stack([r[k] for r in sample_rows]).reshape((depth, dec_batch, n_new) + tail))
    return ((x[:n_prompt].reshape(batch, seq, d), x[n_prompt:].reshape(dec_batch, n_new, d))
            + tuple(prompt_out) + tuple(sample_out))
```
